```python
import math
import jax
import jax.numpy as jnp
from jax import lax
import numpy as np

D_MODEL = 1024
BATCH = 16
SEQ = 2048
DEPTH = 1

CHUNK = 64
Q_BLOCK = 128
N_MEM = 256
EPS = 1e-5
NEG_INF = -1e30

D_MIX = D_MODEL
MLA_HEADS = 8
MLA_NOPE = 64
MLA_ROPE = 32
MLA_V = 64
MLA_Q_RANK = 256
MLA_KV_RANK = 128
ROPE_THETA = 10000.0

SSD_HEADS = 8
SSD_HEADDIM = 64
SSD_INNER = SSD_HEADS * SSD_HEADDIM
SSD_GROUPS = 2
SSD_STATE = 128
SSD_CONV = 4
SSD_CONV_DIM = SSD_INNER + 2 * SSD_GROUPS * SSD_STATE
DT_MIN = 0.001
DT_MAX = 0.1

IN_COLS = MLA_Q_RANK + MLA_KV_RANK + MLA_ROPE + SSD_INNER + SSD_CONV_DIM + SSD_HEADS

XA_HEADS = 4
XA_HEAD_DIM = D_MODEL // XA_HEADS

N_EXPERT_GROUPS = 4
EXPERTS_PER_GROUP = 8
N_EXPERTS = N_EXPERT_GROUPS * EXPERTS_PER_GROUP
TOP_K = 2
D_EXPERT = 256

DEEPNORM_ALPHA = (2.0 * DEPTH) ** 0.25
DEEPNORM_BETA = (8.0 * DEPTH) ** -0.25

kernel_name = "hybrid_mla_ssd_hmoe_deepnorm"


def layer_norm(x, g, b):
    xf = x.astype(jnp.float32)
    mu = jnp.mean(xf, axis=-1, keepdims=True)
    var = jnp.mean(jnp.square(xf - mu), axis=-1, keepdims=True)
    return ((xf - mu) * lax.rsqrt(var + EPS) * g + b).astype(x.dtype)


def rms_norm(x, g):
    xf = x.astype(jnp.float32)
    return (xf * lax.rsqrt(jnp.mean(jnp.square(xf), axis=-1, keepdims=True) + EPS) * g).astype(x.dtype)


def rotary_tables(seq, dim, dtype):
    pos = jnp.arange(seq, dtype=jnp.float32)
    inv_freq = ROPE_THETA ** (-jnp.arange(0, dim, 2, dtype=jnp.float32) / dim)
    ang = pos[:, None] * inv_freq[None, :]
    return jnp.cos(ang).astype(dtype), jnp.sin(ang).astype(dtype)


def apply_rope(u, cos, sin):
    u1, u2 = jnp.split(u, 2, axis=-1)
    return jnp.concatenate([u1 * cos - u2 * sin, u2 * cos + u1 * sin], axis=-1)


def chunk_causal_attention(q, k, v):
    b, s, h, dqk = q.shape
    nb = s // Q_BLOCK
    scale = dqk ** -0.5
    q_blocks = q.reshape(b, nb, Q_BLOCK, h, dqk).transpose(1, 0, 2, 3, 4)
    k_chunk = jnp.arange(s) // CHUNK

    def one_block(args):
        q_blk, blk = args
        q_chunk = (blk * Q_BLOCK + jnp.arange(Q_BLOCK)) // CHUNK
        scores = jnp.einsum("bqhd,bkhd->bhqk", q_blk, k).astype(jnp.float32) * scale
        allowed = k_chunk[None, :] <= q_chunk[:, None]
        scores = jnp.where(allowed, scores, NEG_INF)
        p = jax.nn.softmax(scores, axis=-1).astype(v.dtype)
        return jnp.einsum("bhqk,bkhd->bqhd", p, v)

    out = lax.map(one_block, (q_blocks, jnp.arange(nb)))
    return out.transpose(1, 0, 2, 3, 4).reshape(b, s, h, v.shape[-1])


def causal_depthwise_conv(u, w, bias):
    out = lax.conv_general_dilated(
        u, w[:, None, :], window_strides=(1,), padding=((SSD_CONV - 1, 0),),
        dimension_numbers=("NWC", "WIO", "NWC"), feature_group_count=u.shape[-1])
    return out + bias


def ssd_chunked_scan(xh, dt, a, bm, cm):
    b, s, h, p = xh.shape
    g, n = bm.shape[2], bm.shape[3]
    r = h // g
    nc = s // CHUNK
    x = (xh * dt[..., None]).reshape(b, nc, CHUNK, g, r, p)
    adt = (dt * a).reshape(b, nc, CHUNK, g, r)
    bm = bm.reshape(b, nc, CHUNK, g, n)
    cm = cm.reshape(b, nc, CHUNK, g, n)
    acs = jnp.cumsum(adt, axis=2)
    acs_t = acs.transpose(0, 1, 3, 4, 2)
    seg = acs_t[..., :, None] - acs_t[..., None, :]
    causal = jnp.tril(jnp.ones((CHUNK, CHUNK), dtype=bool))
    decay_in = jnp.exp(jnp.where(causal, seg, -jnp.inf))
    cb = jnp.einsum("bclgn,bcsgn->bcgls", cm, bm)
    y_diag = jnp.einsum("bcgls,bcgrls,bcsgrp->bclgrp", cb, decay_in, x)
    decay_to_end = jnp.exp(acs[:, :, -1:] - acs)
    states = jnp.einsum("bclgn,bclgr,bclgrp->bcgrpn", bm, decay_to_end, x)
    chunk_decay = jnp.exp(acs[:, :, -1])

    def step(state, inp):
        s_c, d_c = inp
        return d_c[..., None, None] * state + s_c, state

    h0 = jnp.zeros((b, g, r, p, n), x.dtype)
    _, prev = lax.scan(step, h0, (states.transpose(1, 0, 2, 3, 4, 5),
                                  chunk_decay.transpose(1, 0, 2, 3)))
    prev = prev.transpose(1, 0, 2, 3, 4, 5)
    y_off = jnp.einsum("bclgn,bcgrpn,bclgr->bclgrp", cm, prev, jnp.exp(acs))
    return (y_diag + y_off).reshape(b, s, h, p)


def hybrid_mixer(x, cos, sin, w_in, mla_q_norm, w_q_up, mla_kv_norm, w_kv_up,
                 ssd_conv_w, ssd_conv_b, ssd_dt_bias, ssd_a_log, ssd_d, ssd_norm, w_out):
    b, s, _ = x.shape
    proj = x @ w_in
    o1 = MLA_Q_RANK
    o2 = o1 + MLA_KV_RANK
    o3 = o2 + MLA_ROPE
    o4 = o3 + SSD_INNER
    o5 = o4 + SSD_CONV_DIM
    c_q, c_kv, k_rope, z, xbc, dt_raw = jnp.split(proj, [o1, o2, o3, o4, o5], axis=-1)

    q = (rms_norm(c_q, mla_q_norm) @ w_q_up).reshape(b, s, MLA_HEADS, MLA_NOPE + MLA_ROPE)
    q_nope, q_pe = q[..., :MLA_NOPE], q[..., MLA_NOPE:]
    kv = (rms_norm(c_kv, mla_kv_norm) @ w_kv_up).reshape(b, s, MLA_HEADS, MLA_NOPE + MLA_V)
    k_nope, v = kv[..., :MLA_NOPE], kv[..., MLA_NOPE:]
    q_pe = apply_rope(q_pe, cos[None, :, None, :], sin[None, :, None, :])
    k_pe = apply_rope(k_rope, cos[None], sin[None])
    q_full = jnp.concatenate([q_nope, q_pe], axis=-1)
    k_full = jnp.concatenate(
        [k_nope, jnp.broadcast_to(k_pe[:, :, None, :], (b, s, MLA_HEADS, MLA_ROPE))], axis=-1)
    attn = chunk_causal_attention(q_full, k_full, v).reshape(b, s, MLA_HEADS * MLA_V)

    xbc = jax.nn.silu(causal_depthwise_conv(xbc, ssd_conv_w, ssd_conv_b))
    xs, bm, cm = jnp.split(xbc, [SSD_INNER, SSD_INNER + SSD_GROUPS * SSD_STATE], axis=-1)
    xh = xs.reshape(b, s, SSD_HEADS, SSD_HEADDIM).astype(jnp.float32)
    dt = jax.nn.softplus(dt_raw.astype(jnp.float32) + ssd_dt_bias)
    a = -jnp.exp(ssd_a_log.astype(jnp.float32))
    y = ssd_chunked_scan(xh, dt, a,
                         bm.reshape(b, s, SSD_GROUPS, SSD_STATE).astype(jnp.float32),
                         cm.reshape(b, s, SSD_GROUPS, SSD_STATE).astype(jnp.float32))
    y = y + ssd_d.astype(jnp.float32)[None, None, :, None] * xh
    yz = (y.reshape(b, s, SSD_INNER) * jax.nn.silu(z.astype(jnp.float32)))
    yz = yz.reshape(b, s, SSD_GROUPS, SSD_INNER // SSD_GROUPS)
    yz = yz * lax.rsqrt(jnp.mean(jnp.square(yz), axis=-1, keepdims=True) + EPS)
    ssd_out = (yz.reshape(b, s, SSD_INNER) * ssd_norm).astype(x.dtype)

    return jnp.concatenate([attn, ssd_out], axis=-1) @ w_out


def memory_cross_attention(x, mem, wq, wk, wv, wo):
    b, s, _ = x.shape
    m = mem.shape[1]
    q = (x @ wq).reshape(b, s, XA_HEADS, XA_HEAD_DIM)
    k = (mem @ wk).reshape(b, m, XA_HEADS, XA_HEAD_DIM)
    v = (mem @ wv).reshape(b, m, XA_HEADS, XA_HEAD_DIM)
    scores = jnp.einsum("bshd,bmhd->bhsm", q, k).astype(jnp.float32) * (XA_HEAD_DIM ** -0.5)
    p = jax.nn.softmax(scores, axis=-1).astype(x.dtype)
    o = jnp.einsum("bhsm,bmhd->bshd", p, v).reshape(b, s, D_MODEL)
    return o @ wo


def hierarchical_moe(x, group_w, group_b, expert_w, expert_b, w_gate, w_up, w_down):
    b, s, d = x.shape
    t = x.reshape(b * s, d)
    n_tok = t.shape[0]
    group_logits = (t @ group_w).astype(jnp.float32) + group_b
    group_probs = jax.nn.softmax(group_logits, axis=-1)
    g_idx = jnp.argmax(group_logits, axis=-1)
    g_gate = jnp.take_along_axis(group_probs, g_idx[:, None], axis=-1)
    e_logits = ((t @ expert_w).astype(jnp.float32) + expert_b).reshape(
        n_tok, N_EXPERT_GROUPS, EXPERTS_PER_GROUP)
    in_group = jnp.take_along_axis(e_logits, g_idx[:, None, None], axis=1)[:, 0]
    top_vals, top_idx = lax.top_k(in_group, TOP_K)
    gates = jax.nn.softmax(top_vals, axis=-1) * g_gate
    expert_id = g_idx[:, None] * EXPERTS_PER_GROUP + top_idx
    combine = jnp.sum(jax.nn.one_hot(expert_id, N_EXPERTS, dtype=jnp.float32)
                      * gates[..., None], axis=1).astype(t.dtype)
    acc = jnp.zeros_like(t)
    for e in range(N_EXPERTS):
        hdn = jax.nn.silu(t @ w_gate[e]) * (t @ w_up[e])
        acc = acc + (hdn @ w_down[e]) * combine[:, e:e + 1]
    return acc.reshape(b, s, d)


def setup_inputs(seed: int = 0) -> dict:
    key = jax.random.key(seed)
    ks = jax.random.split(key, 40)
    L = DEPTH

    def nrm(k, shape, fan_in, scale=1.0):
        return jax.random.normal(k, shape, jnp.float32) * (scale * fan_in ** -0.5)

    def gain(k, n):
        return 1.0 + 0.02 * jax.random.normal(k, (L, n), jnp.float32)

    def small(k, shape):
        return 0.01 * jax.random.normal(k, shape, jnp.float32)

    dt0 = jnp.exp(jax.random.uniform(ks[9], (L, SSD_HEADS), jnp.float32,
                                     minval=math.log(DT_MIN), maxval=math.log(DT_MAX)))
    return {
        "x": jax.random.normal(ks[0], (BATCH, SEQ, D_MODEL), jnp.float32),
        "mem": jax.random.normal(ks[1], (BATCH, N_MEM, D_MODEL), jnp.float32),
        "w_in": nrm(ks[2], (L, D_MODEL, IN_COLS), D_MODEL),
        "mla_q_norm": gain(ks[3], MLA_Q_RANK),
        "w_q_up": nrm(ks[4], (L, MLA_Q_RANK, MLA_HEADS * (MLA_NOPE + MLA_ROPE)), MLA_Q_RANK),
        "mla_kv_norm": gain(ks[5], MLA_KV_RANK),
        "w_kv_up": nrm(ks[6], (L, MLA_KV_RANK, MLA_HEADS * (MLA_NOPE + MLA_V)), MLA_KV_RANK),
        "ssd_conv_w": nrm(ks[7], (L, SSD_CONV, SSD_CONV_DIM), SSD_CONV),
        "ssd_conv_b": small(ks[8], (L, SSD_CONV_DIM)),
        "ssd_dt_bias": dt0 + jnp.log(-jnp.expm1(-dt0)),
        "ssd_a_log": jnp.log(jax.random.uniform(ks[10], (L, SSD_HEADS), jnp.float32,
                                                minval=1.0, maxval=16.0)),
        "ssd_d": 1.0 + 0.1 * jax.random.normal(ks[11], (L, SSD_HEADS), jnp.float32),
        "ssd_norm": gain(ks[12], SSD_INNER),
        "w_out": nrm(ks[13], (L, D_MIX, D_MODEL), D_MIX, DEEPNORM_BETA),
        "ln1_g": gain(ks[14], D_MODEL),
        "ln1_b": small(ks[15], (L, D_MODEL)),
        "xa_wq": nrm(ks[16], (L, D_MODEL, D_MODEL), D_MODEL),
        "xa_wk": nrm(ks[17], (L, D_MODEL, D_MODEL), D_MODEL),
        "xa_wv": nrm(ks[18], (L, D_MODEL, D_MODEL), D_MODEL, DEEPNORM_BETA),
        "xa_wo": nrm(ks[19], (L, D_MODEL, D_MODEL), D_MODEL, DEEPNORM_BETA),
        "ln2_g": gain(ks[20], D_MODEL),
        "ln2_b": small(ks[21], (L, D_MODEL)),
        "router_group_w": nrm(ks[22], (L, D_MODEL, N_EXPERT_GROUPS), D_MODEL),
        "router_group_b": small(ks[23], (L, N_EXPERT_GROUPS)),
        "router_expert_w": nrm(ks[24], (L, D_MODEL, N_EXPERTS), D_MODEL),
        "router_expert_b": small(ks[25], (L, N_EXPERTS)),
        "expert_w_gate": nrm(ks[26], (L, N_EXPERTS, D_MODEL, D_EXPERT), D_MODEL),
        "expert_w_up": nrm(ks[27], (L, N_EXPERTS, D_MODEL, D_EXPERT), D_MODEL),
        "expert_w_down": nrm(ks[28], (L, N_EXPERTS, D_EXPERT, D_MODEL), D_EXPERT, DEEPNORM_BETA),
        "ln3_g": gain(ks[29], D_MODEL),
        "ln3_b": small(ks[30], (L, D_MODEL)),
    }


def reference(x, mem, w_in, mla_q_norm, w_q_up, mla_kv_norm, w_kv_up,
              ssd_conv_w, ssd_conv_b, ssd_dt_bias, ssd_a_log, ssd_d, ssd_norm, w_out,
              ln1_g, ln1_b, xa_wq, xa_wk, xa_wv, xa_wo, ln2_g, ln2_b,
              router_group_w, router_group_b, router_expert_w, router_expert_b,
              expert_w_gate, expert_w_up, expert_w_down, ln3_g, ln3_b):
    cos, sin = rotary_tables(x.shape[1], MLA_ROPE, x.dtype)
    h = x
    for l in range(DEPTH):
        mix = hybrid_mixer(h, cos, sin, w_in[l], mla_q_norm[l], w_q_up[l], mla_kv_norm[l],
                           w_kv_up[l], ssd_conv_w[l], ssd_conv_b[l], ssd_dt_bias[l],
                           ssd_a_log[l], ssd_d[l], ssd_norm[l], w_out[l])
        h = layer_norm(DEEPNORM_ALPHA * h + mix, ln1_g[l], ln1_b[l])
        xa = memory_cross_attention(h, mem, xa_wq[l], xa_wk[l], xa_wv[l], xa_wo[l])
        h = layer_norm(DEEPNORM_ALPHA * h + xa, ln2_g[l], ln2_b[l])
        ffn = hierarchical_moe(h, router_group_w[l], router_group_b[l], router_expert_w[l],
                               router_expert_b[l], expert_w_gate[l], expert_w_up[l],
                               expert_w_down[l])
        h = layer_norm(DEEPNORM_ALPHA * h + ffn, ln3_g[l], ln3_b[l])
    return h
```

```python
import functools
import math

import jax
import jax.numpy as jnp
from jax import lax
from jax.experimental import pallas as pl
from jax.experimental.pallas import tpu as pltpu

F32 = jnp.float32
BF16 = jnp.bfloat16
HIGHEST = lax.Precision.HIGHEST

EPS = 1e-5
NEG_INF = -1e30
CHUNK = 64
ROPE_THETA = 10000.0

D_MODEL = 1024
MLA_HEADS = 8
MLA_NOPE = 64
MLA_ROPE = 32
MLA_V = 64
MLA_Q_RANK = 256
MLA_KV_RANK = 128
SSD_HEADS = 8
SSD_HEADDIM = 64
SSD_INNER = 512
SSD_GROUPS = 2
SSD_STATE = 128
SSD_CONV = 4
SSD_CONV_DIM = 1024
XA_HEADS = 4
XA_HEAD_DIM = 256
N_EXPERT_GROUPS = 4
EXPERTS_PER_GROUP = 8
N_EXPERTS = 32
D_EXPERT = 256
DEPTH = 1
ALPHA = (2.0 * DEPTH) ** 0.25

LANES = 128
V7X_VMEM_LIMIT = 56 * 1024 * 1024

C_CQ = 0
C_CKV = 256
C_Z = 384
C_XBC = 896
C_KPE = 1920
C_KPER = 2048
C_DT = 2176
IN_COLS_R = 2304

TM_IN = 256
Q_BLK = 256
SSD_L = 256
TM_MID = 256
TM_MOE = 256
PERM_CHUNK = 2048
META = 128
ROW_W = D_MODEL + META


def _cparams(sem):
    return pltpu.CompilerParams(dimension_semantics=sem, vmem_limit_bytes=V7X_VMEM_LIMIT)


def _sigmoid(x):
    return 1.0 / (1.0 + jnp.exp(-x))


def _layer_norm(x, g, b):
    mu = jnp.mean(x, axis=-1, keepdims=True)
    xc = x - mu
    var = jnp.mean(xc * xc, axis=-1, keepdims=True)
    return xc * lax.rsqrt(var + EPS) * g + b


def _fold_kernel(wqn_ref, wukt_ref, wuv_ref, woa_ref, gkv_row_ref, gkv_col_ref, wabs_ref, wof_ref, *, scale):
    for h in range(MLA_HEADS):
        wabs_ref[h] = scale * jnp.dot(wqn_ref[h], wukt_ref[h] * gkv_row_ref[...],
                                      precision=HIGHEST, preferred_element_type=F32)
        wof_ref[h] = jnp.dot(wuv_ref[h] * gkv_col_ref[...], woa_ref[h],
                             precision=HIGHEST, preferred_element_type=F32)


def _fold_weights(w_q_up, w_kv_up, mla_kv_norm, w_out, scale):
    wq = w_q_up.reshape(MLA_Q_RANK, MLA_HEADS, MLA_NOPE + MLA_ROPE)
    wkv = w_kv_up.reshape(MLA_KV_RANK, MLA_HEADS, MLA_NOPE + MLA_V)
    wqn = jnp.transpose(wq[:, :, :MLA_NOPE], (1, 0, 2))
    wukt = jnp.transpose(wkv[:, :, :MLA_NOPE], (1, 2, 0))
    wuv = jnp.transpose(wkv[:, :, MLA_NOPE:], (1, 0, 2))
    woa = w_out[:MLA_HEADS * MLA_V].reshape(MLA_HEADS, MLA_V, D_MODEL)
    wabs, wof = pl.pallas_call(
        functools.partial(_fold_kernel, scale=scale),
        out_shape=(jax.ShapeDtypeStruct((MLA_HEADS, MLA_Q_RANK, MLA_KV_RANK), F32),
                   jax.ShapeDtypeStruct((MLA_HEADS, MLA_KV_RANK, D_MODEL), F32)),
        name="fold",
    )(wqn, wukt, wuv, woa, mla_kv_norm.reshape(1, MLA_KV_RANK), mla_kv_norm.reshape(MLA_KV_RANK, 1))
    w_abs = jnp.transpose(wabs, (1, 0, 2)).reshape(MLA_Q_RANK, MLA_HEADS * MLA_KV_RANK).astype(BF16)
    w_pe3 = wq[:, :, MLA_NOPE:] * scale
    half = MLA_ROPE // 2
    w_per3 = jnp.concatenate([-w_pe3[:, :, half:], w_pe3[:, :, :half]], axis=-1)
    w_pe = w_pe3.reshape(MLA_Q_RANK, MLA_HEADS * MLA_ROPE).astype(BF16)
    w_per = w_per3.reshape(MLA_Q_RANK, MLA_HEADS * MLA_ROPE).astype(BF16)
    w_mix = jnp.concatenate([wof.reshape(MLA_HEADS * MLA_KV_RANK, D_MODEL), w_out[MLA_HEADS * MLA_V:]],
                            axis=0).astype(BF16)
    return w_abs, w_pe, w_per, w_mix


def _inproj_kernel(x_ref, w_ref, gq_ref, cw_ref, cb_ref, dtb_ref, cos_ref, sin_ref,
                   cq_ref, kc_ref, z_ref, xbc_ref, dt_ref, cbuf):
    j = pl.program_id(1)
    tm = x_ref.shape[0]
    proj = jnp.dot(x_ref[...].astype(BF16), w_ref[...], preferred_element_type=F32)

    c_q = proj[:, C_CQ:C_CQ + MLA_Q_RANK]
    cq = c_q * lax.rsqrt(jnp.mean(c_q * c_q, axis=-1, keepdims=True) + EPS) * gq_ref[...]
    cq_ref[...] = cq.astype(BF16)

    c_kv = proj[:, C_CKV:C_CKV + MLA_KV_RANK]
    ckv = c_kv * lax.rsqrt(jnp.mean(c_kv * c_kv, axis=-1, keepdims=True) + EPS)
    kpe = proj[:, C_KPE:C_KPE + LANES] * cos_ref[...] + proj[:, C_KPER:C_KPER + LANES] * sin_ref[...]
    kc_ref[...] = jnp.concatenate([ckv, kpe], axis=1).astype(BF16)

    z_ref[...] = proj[:, C_Z:C_Z + SSD_INNER].astype(BF16)

    @pl.when(j == 0)
    def _():
        cbuf[0:8, :] = jnp.zeros((8, SSD_CONV_DIM), F32)

    cbuf[8:8 + tm, :] = proj[:, C_XBC:C_XBC + SSD_CONV_DIM]
    acc = cb_ref[...] + cw_ref[SSD_CONV - 1:SSD_CONV, :] * proj[:, C_XBC:C_XBC + SSD_CONV_DIM]
    for k in range(SSD_CONV - 1):
        acc = acc + cw_ref[k:k + 1, :] * cbuf[pl.ds(8 - (SSD_CONV - 1) + k, tm), :]
    xbc_ref[...] = (acc * _sigmoid(acc)).astype(BF16)
    cbuf[0:8, :] = cbuf[tm:tm + 8, :]

    dtr = proj[:, C_DT:C_DT + LANES] + dtb_ref[...]
    dt_ref[...] = jnp.maximum(dtr, 0.0) + jnp.log(1.0 + jnp.exp(-jnp.abs(dtr)))


def _inproj(x, w_in_r, gq, conv_w, conv_b, dt_bias_p, cos4, sin4):
    b, s, d = x.shape
    tm = TM_IN
    grid = (b, s // tm)
    tok = lambda w: pl.BlockSpec((None, tm, w), lambda bi, j: (bi, j, 0))
    full = lambda a: pl.BlockSpec(a.shape, lambda bi, j: (0,) * a.ndim)
    pos = pl.BlockSpec((tm, LANES), lambda bi, j: (j, 0))
    return pl.pallas_call(
        _inproj_kernel,
        grid=grid,
        in_specs=[tok(d), full(w_in_r), full(gq), full(conv_w), full(conv_b), full(dt_bias_p), pos, pos],
        out_specs=(tok(MLA_Q_RANK), tok(2 * LANES), tok(SSD_INNER), tok(SSD_CONV_DIM), tok(LANES)),
        out_shape=(jax.ShapeDtypeStruct((b, s, MLA_Q_RANK), BF16),
                   jax.ShapeDtypeStruct((b, s, 2 * LANES), BF16),
                   jax.ShapeDtypeStruct((b, s, SSD_INNER), BF16),
                   jax.ShapeDtypeStruct((b, s, SSD_CONV_DIM), BF16),
                   jax.ShapeDtypeStruct((b, s, LANES), F32)),
        scratch_shapes=[pltpu.VMEM((tm + 8, SSD_CONV_DIM), F32)],
        compiler_params=_cparams(("arbitrary", "arbitrary")),
        name="inproj",
    )(x, w_in_r, gq, conv_w, conv_b, dt_bias_p, cos4, sin4)


def _attn_kernel(cq_ref, kc_ref, wabs_ref, wpe_ref, wper_ref, cos_ref, sin_ref, o_ref,
                 q_scr, m_scr, l_scr, acc_scr):
    i = pl.program_id(1)
    qb = cq_ref.shape[0]
    rows = MLA_HEADS * qb
    cq = cq_ref[...]
    qabs = jnp.dot(cq, wabs_ref[...], preferred_element_type=F32)
    cos8 = jnp.concatenate([cos_ref[...], cos_ref[...]], axis=1)
    sin8 = jnp.concatenate([sin_ref[...], sin_ref[...]], axis=1)
    qpe = (jnp.dot(cq, wpe_ref[...], preferred_element_type=F32) * cos8
           + jnp.dot(cq, wper_ref[...], preferred_element_type=F32) * sin8)
    lane = lax.broadcasted_iota(jnp.int32, (qb, LANES), 1)
    heads_per_blk = LANES // MLA_ROPE
    for h in range(MLA_HEADS):
        pe_blk = qpe[:, LANES * (h // heads_per_blk):LANES * (h // heads_per_blk + 1)]
        pe_h = jnp.where(lane // MLA_ROPE == h % heads_per_blk, pe_blk, 0.0)
        q_scr[h * qb:(h + 1) * qb, :] = jnp.concatenate(
            [qabs[:, h * MLA_KV_RANK:(h + 1) * MLA_KV_RANK], pe_h], axis=1).astype(BF16)

    m_scr[...] = jnp.full((rows, 1), -jnp.inf, F32)
    l_scr[...] = jnp.zeros((rows, 1), F32)
    acc_scr[...] = jnp.zeros((rows, MLA_KV_RANK), F32)

    def update(j, masked):
        kblk = kc_ref[pl.ds(pl.multiple_of(j * qb, qb), qb), :]
        s = lax.dot_general(q_scr[...], kblk, (((1,), (1,)), ((), ())), preferred_element_type=F32)
        if masked:
            r = lax.broadcasted_iota(jnp.int32, (rows, qb), 0)
            c = lax.broadcasted_iota(jnp.int32, (rows, qb), 1)
            s = jnp.where(c // CHUNK <= (r % qb) // CHUNK, s, NEG_INF)
        m_old = m_scr[...]
        m_new = jnp.maximum(m_old, jnp.max(s, axis=-1, keepdims=True))
        p = jnp.exp(s - m_new)
        a = jnp.exp(m_old - m_new)
        l_scr[...] = a * l_scr[...] + jnp.sum(p, axis=-1, keepdims=True)
        acc_scr[...] = a * acc_scr[...] + jnp.dot(p.astype(BF16), kblk[:, :MLA_KV_RANK],
                                                  preferred_element_type=F32)
        m_scr[...] = m_new

    def body(j, carry):
        update(j, False)
        return carry

    lax.fori_loop(0, i, body, 0)
    update(i, True)

    o = acc_scr[...] / l_scr[...]
    for h in range(MLA_HEADS):
        o_ref[:, h * MLA_KV_RANK:(h + 1) * MLA_KV_RANK] = o[h * qb:(h + 1) * qb, :].astype(BF16)


def _attention(cq, kc, w_abs, w_pe, w_per, cos4, sin4):
    b, s, _ = cq.shape
    qb = Q_BLK
    rows = MLA_HEADS * qb
    full = lambda a: pl.BlockSpec(a.shape, lambda bi, i: (0,) * a.ndim)
    pos = pl.BlockSpec((qb, LANES), lambda bi, i: (i, 0))
    return pl.pallas_call(
        _attn_kernel,
        grid=(b, s // qb),
        in_specs=[pl.BlockSpec((None, qb, MLA_Q_RANK), lambda bi, i: (bi, i, 0)),
                  pl.BlockSpec((None, s, 2 * LANES), lambda bi, i: (bi, 0, 0)),
                  full(w_abs), full(w_pe), full(w_per), pos, pos],
        out_specs=pl.BlockSpec((None, qb, MLA_HEADS * MLA_KV_RANK), lambda bi, i: (bi, i, 0)),
        out_shape=jax.ShapeDtypeStruct((b, s, MLA_HEADS * MLA_KV_RANK), BF16),
        scratch_shapes=[pltpu.VMEM((rows, 2 * LANES), BF16),
                        pltpu.VMEM((rows, 1), F32),
                        pltpu.VMEM((rows, 1), F32),
                        pltpu.VMEM((rows, MLA_KV_RANK), F32)],
        compiler_params=_cparams(("arbitrary", "arbitrary")),
        name="attn",
    )(cq, kc, w_abs, w_pe, w_per, cos4, sin4)


def _ssd_kernel(xbc_ref, dt_ref, z_ref, alog_ref, dfull_ref, norm_ref, e_ref, o_ref, state_scr):
    c = pl.program_id(1)
    ln = xbc_ref.shape[0]
    gw = SSD_INNER // SSD_GROUPS
    hpg = SSD_HEADS // SSD_GROUPS

    @pl.when(c == 0)
    def _():
        state_scr[...] = jnp.zeros(state_scr.shape, F32)

    xs = xbc_ref[:, 0:SSD_INNER].astype(F32)
    dt = dt_ref[...]
    lane1 = lax.broadcasted_iota(jnp.int32, (1, LANES), 1)
    a = jnp.where(lane1 < SSD_HEADS, -jnp.exp(alog_ref[...]), 0.0)
    adt = dt * a
    row = lax.broadcasted_iota(jnp.int32, (ln, ln), 0)
    col = lax.broadcasted_iota(jnp.int32, (ln, ln), 1)
    causal = col <= row
    acs = jnp.dot(causal.astype(F32), adt, precision=HIGHEST, preferred_element_type=F32)
    e = e_ref[...]
    acs_e = jnp.dot(acs, e, precision=HIGHEST, preferred_element_type=F32)
    dt_e = jnp.dot(dt, e, precision=HIGHEST, preferred_element_type=F32)
    acs_end = acs_e[ln - 1:ln, :]
    xdt = xs * dt_e
    x_end = (xdt * jnp.exp(acs_end - acs_e)).astype(BF16)
    eacs = jnp.exp(acs_e)
    chunk_decay = jnp.exp(acs_end)
    acs_t = acs.T
    lane_g = lax.broadcasted_iota(jnp.int32, (ln, gw), 1)

    ys = []
    for g in range(SSD_GROUPS):
        bg = xbc_ref[:, SSD_INNER + g * SSD_STATE:SSD_INNER + (g + 1) * SSD_STATE]
        cg = xbc_ref[:, SSD_INNER + SSD_GROUPS * SSD_STATE + g * SSD_STATE:
                     SSD_INNER + SSD_GROUPS * SSD_STATE + (g + 1) * SSD_STATE]
        cb = lax.dot_general(cg, bg, (((1,), (1,)), ((), ())), preferred_element_type=F32)
        prev = state_scr[g]
        y = jnp.dot(cg, prev.astype(BF16), preferred_element_type=F32) * eacs[:, g * gw:(g + 1) * gw]
        xg = xdt[:, g * gw:(g + 1) * gw]
        for hh in range(hpg):
            h = g * hpg + hh
            seg = acs[:, h:h + 1] - acs_t[h:h + 1, :]
            dec = jnp.exp(jnp.where(causal, seg, -jnp.inf))
            xm = jnp.where(lane_g // SSD_HEADDIM == hh, xg, 0.0).astype(BF16)
            y = y + jnp.dot((cb * dec).astype(BF16), xm, preferred_element_type=F32)
        st = lax.dot_general(bg, x_end[:, g * gw:(g + 1) * gw], (((0,), (0,)), ((), ())),
                             preferred_element_type=F32)
        state_scr[g] = chunk_decay[:, g * gw:(g + 1) * gw] * prev + st
        ys.append(y)

    y = jnp.concatenate(ys, axis=1) + dfull_ref[...] * xs
    zf = z_ref[...].astype(F32)
    yz = y * (zf * _sigmoid(zf))
    outs = []
    for g in range(SSD_GROUPS):
        yg = yz[:, g * gw:(g + 1) * gw]
        outs.append(yg * lax.rsqrt(jnp.mean(yg * yg, axis=-1, keepdims=True) + EPS))
    o_ref[...] = (jnp.concatenate(outs, axis=1) * norm_ref[...]).astype(BF16)


def _ssd(xbc, dt, z, alog_p, d_full, ssd_norm, expand):
    b, s, _ = xbc.shape
    ln = SSD_L
    tok = lambda w: pl.BlockSpec((None, ln, w), lambda bi, c: (bi, c, 0))
    full = lambda a: pl.BlockSpec(a.shape, lambda bi, c: (0,) * a.ndim)
    return pl.pallas_call(
        _ssd_kernel,
        grid=(b, s // ln),
        in_specs=[tok(SSD_CONV_DIM), tok(LANES), tok(SSD_INNER), full(alog_p), full(d_full), full(ssd_norm),
                  full(expand)],
        out_specs=tok(SSD_INNER),
        out_shape=jax.ShapeDtypeStruct((b, s, SSD_INNER), BF16),
        scratch_shapes=[pltpu.VMEM((SSD_GROUPS, SSD_STATE, SSD_INNER // SSD_GROUPS), F32)],
        compiler_params=_cparams(("arbitrary", "arbitrary")),
        name="ssd",
    )(xbc, dt, z, alog_p, d_full, ssd_norm, expand)


def _memkv_kernel(mem_ref, wk_ref, wv_ref, k_ref, v_ref):
    m = mem_ref[...].astype(BF16)
    k_ref[...] = jnp.dot(m, wk_ref[...], preferred_element_type=F32).astype(BF16)
    v_ref[...] = jnp.dot(m, wv_ref[...], preferred_element_type=F32).astype(BF16)


def _memkv(mem, wk, wv):
    b, m, d = mem.shape
    blk = pl.BlockSpec((None, m, d), lambda bi: (bi, 0, 0))
    full = lambda a: pl.BlockSpec(a.shape, lambda bi: (0,) * a.ndim)
    return pl.pallas_call(
        _memkv_kernel,
        grid=(b,),
        in_specs=[blk, full(wk), full(wv)],
        out_specs=(blk, blk),
        out_shape=(jax.ShapeDtypeStruct((b, m, d), BF16), jax.ShapeDtypeStruct((b, m, d), BF16)),
        compiler_params=_cparams(("arbitrary",)),
        name="memkv",
    )(mem, wk, wv)


def _mid_kernel(x_ref, ol_ref, ss_ref, wmix_ref, g1_ref, b1_ref, wq_ref, km_ref, vm_ref, wo_ref,
                g2_ref, b2_ref, wr_ref, br_ref, row_ref, meta_ref, cnt_ref, run_scr):
    step = pl.program_id(0) * pl.num_programs(1) + pl.program_id(1)
    tm = x_ref.shape[0]
    n_lat = MLA_HEADS * MLA_KV_RANK

    @pl.when(step == 0)
    def _():
        run_scr[...] = jnp.zeros(run_scr.shape, F32)

    mix = (jnp.dot(ol_ref[...], wmix_ref[0:n_lat, :], preferred_element_type=F32)
           + jnp.dot(ss_ref[...], wmix_ref[n_lat:, :], preferred_element_type=F32))
    h1 = _layer_norm(ALPHA * x_ref[...] + mix, g1_ref[...], b1_ref[...])

    q = jnp.dot(h1.astype(BF16), wq_ref[...], preferred_element_type=F32).astype(BF16)
    outs = []
    for h in range(XA_HEADS):
        sl = slice(h * XA_HEAD_DIM, (h + 1) * XA_HEAD_DIM)
        s = lax.dot_general(q[:, sl], km_ref[:, sl], (((1,), (1,)), ((), ())), preferred_element_type=F32)
        p = jnp.exp(s - jnp.max(s, axis=-1, keepdims=True))
        o = jnp.dot(p.astype(BF16), vm_ref[:, sl], preferred_element_type=F32)
        outs.append((o / jnp.sum(p, axis=-1, keepdims=True)).astype(BF16))
    xa = jnp.dot(jnp.concatenate(outs, axis=1), wo_ref[...], preferred_element_type=F32)
    h2 = _layer_norm(ALPHA * h1 + xa, g2_ref[...], b2_ref[...])

    logits = jnp.dot(h2, wr_ref[...], precision=HIGHEST, preferred_element_type=F32) + br_ref[...]
    lane = lax.broadcasted_iota(jnp.int32, (tm, LANES), 1)
    big = jnp.int32(LANES)
    is_g = (lane >= N_EXPERTS) & (lane < N_EXPERTS + N_EXPERT_GROUPS)
    gmax = jnp.max(jnp.where(is_g, logits, -jnp.inf), axis=-1, keepdims=True)
    g_idx = jnp.min(jnp.where(is_g & (logits == gmax), lane - N_EXPERTS, big), axis=-1, keepdims=True)
    g_gate = 1.0 / jnp.sum(jnp.where(is_g, jnp.exp(logits - gmax), 0.0), axis=-1, keepdims=True)
    in_grp = (lane < N_EXPERTS) & (lane // EXPERTS_PER_GROUP == g_idx)
    t1 = jnp.max(jnp.where(in_grp, logits, -jnp.inf), axis=-1, keepdims=True)
    i1 = jnp.min(jnp.where(in_grp & (logits == t1), lane, big), axis=-1, keepdims=True)
    rest = in_grp & (lane != i1)
    t2 = jnp.max(jnp.where(rest, logits, -jnp.inf), axis=-1, keepdims=True)
    i2 = jnp.min(jnp.where(rest & (logits == t2), lane, big), axis=-1, keepdims=True)
    e21 = jnp.exp(t2 - t1)
    w1 = g_gate / (1.0 + e21)
    w2 = w1 * e21
    comb = jnp.where(lane == i1, w1, 0.0) + jnp.where(lane == i2, w2, 0.0)
    c8 = comb
    for k in range(1, N_EXPERT_GROUPS):
        c8 = c8 + pltpu.roll(comb, LANES - k * EXPERTS_PER_GROUP, 1)
    c8 = jnp.where(lane < EXPERTS_PER_GROUP, c8, 0.0)

    onehot = (lane == g_idx).astype(BF16)
    r = lax.broadcasted_iota(jnp.int32, (tm, tm), 0)
    c = lax.broadcasted_iota(jnp.int32, (tm, tm), 1)
    before = jnp.dot((c < r).astype(BF16), onehot, preferred_element_type=F32) + run_scr[...]
    rank = jnp.sum(jnp.where(lane == g_idx, before, 0.0), axis=-1, keepdims=True)
    run_scr[...] = run_scr[...] + jnp.sum(onehot.astype(F32), axis=0, keepdims=True)
    cnt_ref[...] = run_scr[...]

    row_ref[:, 0:D_MODEL] = h2
    row_ref[:, D_MODEL:ROW_W] = c8
    meta_ref[...] = jnp.where(lane == 0, g_idx.astype(F32), jnp.where(lane == 1, rank, 0.0))


def _mid(x, o_lat, ssd_out, w_mix, g1, b1, wq, kmem, vmem, wo, g2, b2, wr, br):
    b, s, d = x.shape
    tm = TM_MID
    tok = lambda w: pl.BlockSpec((None, tm, w), lambda bi, j: (bi, j, 0))
    full = lambda a: pl.BlockSpec(a.shape, lambda bi, j: (0,) * a.ndim)
    mem = pl.BlockSpec((None,) + kmem.shape[1:], lambda bi, j: (bi, 0, 0))
    return pl.pallas_call(
        _mid_kernel,
        grid=(b, s // tm),
        in_specs=[tok(d), tok(o_lat.shape[-1]), tok(SSD_INNER), full(w_mix), full(g1), full(b1), full(wq),
                  mem, mem, full(wo), full(g2), full(b2), full(wr), full(br)],
        out_specs=(tok(ROW_W), tok(LANES), pl.BlockSpec((1, LANES), lambda bi, j: (0, 0))),
        out_shape=(jax.ShapeDtypeStruct((b, s, ROW_W), F32),
                   jax.ShapeDtypeStruct((b, s, LANES), F32),
                   jax.ShapeDtypeStruct((1, LANES), F32)),
        scratch_shapes=[pltpu.VMEM((1, LANES), F32)],
        compiler_params=_cparams(("arbitrary", "arbitrary")),
        name="mid",
    )(x, o_lat, ssd_out, w_mix, g1, b1, wq, kmem, vmem, wo, g2, b2, wr, br)


def _scatter_kernel(dest_ref, padrow_ref, rows_hbm, out_hbm, zero_scr, sem, psem):
    c = pl.program_id(0)
    n = dest_ref.shape[1]
    base = c * n

    def issue(t, carry):
        pltpu.make_async_copy(rows_hbm.at[base + t], out_hbm.at[dest_ref[0, t]], sem).start()
        return carry

    lax.fori_loop(0, n, issue, 0)

    npad = padrow_ref.shape[1]

    @pl.when(c == 0)
    def _():
        zero_scr[...] = jnp.zeros(zero_scr.shape, F32)

        def issue_pad(k, carry):
            pltpu.make_async_copy(zero_scr.at[0], out_hbm.at[padrow_ref[0, k]], psem).start()
            return carry

        lax.fori_loop(0, npad, issue_pad, 0)

        def wait_pad(k, carry):
            pltpu.make_async_copy(zero_scr.at[0], out_hbm.at[0], psem).wait()
            return carry

        lax.fori_loop(0, npad, wait_pad, 0)

    pltpu.make_async_copy(rows_hbm.at[pl.ds(0, n)], out_hbm.at[pl.ds(0, n)], sem).wait()


def _scatter_rows(rows, dest2, padrow, n_out):
    t, w = rows.shape
    n = dest2.shape[-1]
    return pl.pallas_call(
        _scatter_kernel,
        grid=(t // n,),
        in_specs=[pl.BlockSpec((None, 1, n), lambda c: (c, 0, 0), memory_space=pltpu.SMEM),
                  pl.BlockSpec(padrow.shape, lambda c: (0, 0), memory_space=pltpu.SMEM),
                  pl.BlockSpec(memory_space=pl.ANY)],
        out_specs=pl.BlockSpec(memory_space=pl.ANY),
        out_shape=jax.ShapeDtypeStruct((n_out, w), rows.dtype),
        scratch_shapes=[pltpu.VMEM((8, w), F32), pltpu.SemaphoreType.DMA(()), pltpu.SemaphoreType.DMA(())],
        compiler_params=_cparams(("arbitrary",)),
        name="scatter",
    )(dest2, padrow, rows)


def _unsort_kernel(dest_ref, rows_hbm, out_hbm, sem):
    c = pl.program_id(0)
    n = dest_ref.shape[1]
    base = c * n

    def issue(t, carry):
        pltpu.make_async_copy(rows_hbm.at[dest_ref[0, t]], out_hbm.at[base + t], sem).start()
        return carry

    lax.fori_loop(0, n, issue, 0)
    pltpu.make_async_copy(rows_hbm.at[pl.ds(0, n)], out_hbm.at[pl.ds(0, n)], sem).wait()


def _unsort_rows(rows, dest2, n_tok):
    _, w = rows.shape
    n = dest2.shape[-1]
    return pl.pallas_call(
        _unsort_kernel,
        grid=(n_tok // n,),
        in_specs=[pl.BlockSpec((None, 1, n), lambda c: (c, 0, 0), memory_space=pltpu.SMEM),
                  pl.BlockSpec(memory_space=pl.ANY)],
        out_specs=pl.BlockSpec(memory_space=pl.ANY),
        out_shape=jax.ShapeDtypeStruct((n_tok, w), rows.dtype),
        scratch_shapes=[pltpu.SemaphoreType.DMA(())],
        compiler_params=_cparams(("arbitrary",)),
        name="unsort",
    )(dest2, rows)


def _moe_kernel(gid_ref, nused_ref, rows_ref, wg_ref, wu_ref, wd_ref, g3_ref, b3_ref, o_ref):
    i = pl.program_id(0)

    @pl.when(i < nused_ref[0])
    def _():
        x = rows_ref[:, 0:D_MODEL]
        xb = x.astype(BF16)
        acc = jnp.zeros(x.shape, F32)
        for e in range(EXPERTS_PER_GROUP):
            hg = jnp.dot(xb, wg_ref[e], preferred_element_type=F32)
            hu = jnp.dot(xb, wu_ref[e], preferred_element_type=F32)
            hd = hg * _sigmoid(hg) * hu * rows_ref[:, D_MODEL + e:D_MODEL + e + 1]
            acc = acc + jnp.dot(hd.astype(BF16), wd_ref[e], preferred_element_type=F32)
        o_ref[...] = _layer_norm(ALPHA * x + acc, g3_ref[...], b3_ref[...])

    @pl.when(i >= nused_ref[0])
    def _():
        o_ref[...] = jnp.zeros(o_ref.shape, F32)


def _moe(rows_sorted, tile_gid, n_used, wg, wu, wd, g3, b3):
    n_rows, w = rows_sorted.shape
    tm = TM_MOE
    nt = n_rows // tm

    def tile_map(i, gid, nused):
        return (jnp.minimum(i, nused[0] - 1), 0)

    def w_map(i, gid, nused):
        return (gid[i], 0, 0)

    full = lambda a: pl.BlockSpec(a.shape, lambda i, gid, nused: (0,) * a.ndim)
    e = EXPERTS_PER_GROUP
    grid_spec = pltpu.PrefetchScalarGridSpec(
        num_scalar_prefetch=2,
        grid=(nt,),
        in_specs=[pl.BlockSpec((tm, w), tile_map),
                  pl.BlockSpec((e, D_MODEL, D_EXPERT), w_map),
                  pl.BlockSpec((e, D_MODEL, D_EXPERT), w_map),
                  pl.BlockSpec((e, D_EXPERT, D_MODEL), w_map),
                  full(g3), full(b3)],
        out_specs=pl.BlockSpec((tm, D_MODEL), lambda i, gid, nused: (i, 0)),
    )
    return pl.pallas_call(
        _moe_kernel,
        grid_spec=grid_spec,
        out_shape=jax.ShapeDtypeStruct((n_rows, D_MODEL), F32),
        compiler_params=_cparams(("arbitrary",)),
        name="moe",
    )(tile_gid, n_used, rows_sorted, wg, wu, wd, g3, b3)


def _rope_tables(seq):
    pos = jnp.arange(seq, dtype=F32)
    inv_freq = ROPE_THETA ** (-jnp.arange(0, MLA_ROPE, 2, dtype=F32) / MLA_ROPE)
    ang = pos[:, None] * inv_freq[None, :]
    cos = jnp.tile(jnp.cos(ang), (1, LANES // (MLA_ROPE // 2)))
    sin = jnp.tile(jnp.sin(ang), (1, LANES // (MLA_ROPE // 2)))
    return cos, sin


def _pad_lanes(v, n=LANES):
    v = v.reshape(1, -1)
    return jnp.pad(v, ((0, 0), (0, n - v.shape[1])))


def kernel(x, mem, w_in, mla_q_norm, w_q_up, mla_kv_norm, w_kv_up, ssd_conv_w, ssd_conv_b, ssd_dt_bias, ssd_a_log, ssd_d, ssd_norm, w_out, ln1_g, ln1_b, xa_wq, xa_wk, xa_wv, xa_wo, ln2_g, ln2_b, router_group_w, router_group_b, router_expert_w, router_expert_b, expert_w_gate, expert_w_up, expert_w_down, ln3_g, ln3_b):
    b, s, d = x.shape
    t = b * s
    l = 0
    row = lambda v: v[l].reshape(1, -1)

    wi = w_in[l]
    o1 = MLA_Q_RANK
    o2 = o1 + MLA_KV_RANK
    o3 = o2 + MLA_ROPE
    o4 = o3 + SSD_INNER
    o5 = o4 + SSD_CONV_DIM
    w_kr = wi[:, o2:o3]
    half = MLA_ROPE // 2
    w_kr_rot = jnp.concatenate([-w_kr[:, half:], w_kr[:, :half]], axis=1)
    reps = LANES // MLA_ROPE
    w_in_r = jnp.concatenate(
        [wi[:, :o2], wi[:, o3:o4], wi[:, o4:o5], jnp.tile(w_kr, (1, reps)), jnp.tile(w_kr_rot, (1, reps)),
         wi[:, o5:], jnp.zeros((d, LANES - SSD_HEADS), F32)], axis=1).astype(BF16)
    cos4, sin4 = _rope_tables(s)
    attn_scale = (MLA_NOPE + MLA_ROPE) ** -0.5
    w_abs, w_pe, w_per, w_mix = _fold_weights(w_q_up[l], w_kv_up[l], mla_kv_norm[l], w_out[l], attn_scale)
    expand = (jnp.arange(LANES)[:, None] == (jnp.arange(SSD_INNER)[None, :] // SSD_HEADDIM)).astype(F32)
    d_full = jnp.repeat(ssd_d[l], SSD_HEADDIM).reshape(1, SSD_INNER)
    wr = jnp.concatenate([router_expert_w[l], router_group_w[l],
                          jnp.zeros((d, LANES - N_EXPERTS - N_EXPERT_GROUPS), F32)], axis=1)
    br = _pad_lanes(jnp.concatenate([router_expert_b[l], router_group_b[l]]))

    cq, kc, z, xbc, dt = _inproj(x, w_in_r, row(mla_q_norm), ssd_conv_w[l], row(ssd_conv_b),
                                 _pad_lanes(ssd_dt_bias[l]), cos4, sin4)
    o_lat = _attention(cq, kc, w_abs, w_pe, w_per, cos4, sin4)
    ssd_out = _ssd(xbc, dt, z, _pad_lanes(ssd_a_log[l]), d_full, row(ssd_norm), expand)

    kmem, vmem = _memkv(mem, xa_wk[l].astype(BF16), xa_wv[l].astype(BF16))
    wq = (xa_wq[l] * (XA_HEAD_DIM ** -0.5)).astype(BF16)
    rows, meta, counts = _mid(x, o_lat, ssd_out, w_mix, row(ln1_g), row(ln1_b), wq, kmem, vmem,
                              xa_wo[l].astype(BF16), row(ln2_g), row(ln2_b), wr, br)

    tm = TM_MOE
    nt = t // tm + N_EXPERT_GROUPS - 1
    meta = meta.reshape(t, LANES)
    g_idx = meta[:, 0].astype(jnp.int32)
    rank = meta[:, 1].astype(jnp.int32)
    cnt = counts[0, :N_EXPERT_GROUPS].astype(jnp.int32)
    ntile = (cnt + tm - 1) // tm
    tile_end = jnp.cumsum(ntile)
    tile_start = tile_end - ntile
    dest = tile_start[g_idx] * tm + rank
    n_used = tile_end[-1:].astype(jnp.int32)
    tile_gid = jnp.minimum(jnp.sum(jnp.arange(nt)[:, None] >= tile_end[None, :], axis=1),
                           N_EXPERT_GROUPS - 1).astype(jnp.int32)
    pad_end = jnp.cumsum(ntile * tm - cnt)
    k = jnp.arange(nt * tm - t)
    kg = jnp.sum(k[:, None] >= pad_end[None, :], axis=1)
    kgc = jnp.minimum(kg, N_EXPERT_GROUPS - 1)
    off = k - jnp.where(kg > 0, pad_end[jnp.maximum(kg, 1) - 1], 0)
    padrow = jnp.where(kg < N_EXPERT_GROUPS, tile_start[kgc] * tm + cnt[kgc] + off,
                       tile_end[-1] * tm + off).astype(jnp.int32).reshape(1, -1)
    chunk = min(PERM_CHUNK, t)
    dest2 = dest.reshape(t // chunk, 1, chunk)

    rows_sorted = _scatter_rows(rows.reshape(t, ROW_W), dest2, padrow, nt * tm)
    out_sorted = _moe(rows_sorted, tile_gid, n_used, expert_w_gate[l].astype(BF16),
                      expert_w_up[l].astype(BF16), expert_w_down[l].astype(BF16), row(ln3_g), row(ln3_b))
    out = _unsort_rows(out_sorted, dest2, t)
    return out.reshape(b, s, d)
```

```python
import functools
import math

import jax
import jax.numpy as jnp
from jax import lax
from jax.experimental import pallas as pl
from jax.experimental.pallas import tpu as pltpu

F32 = jnp.float32
BF16 = jnp.bfloat16
HIGHEST = lax.Precision.HIGHEST

EPS = 1e-5
NEG_INF = -1e30
CHUNK = 64
ROPE_THETA = 10000.0

D_MODEL = 1024
MLA_HEADS = 8
MLA_NOPE = 64
MLA_ROPE = 32
MLA_V = 64
MLA_Q_RANK = 256
MLA_KV_RANK = 128
SSD_HEADS = 8
SSD_HEADDIM = 64
SSD_INNER = 512
SSD_GROUPS = 2
SSD_STATE = 128
SSD_CONV = 4
SSD_CONV_DIM = 1024
XA_HEADS = 4
XA_HEAD_DIM = 256
N_EXPERT_GROUPS = 4
EXPERTS_PER_GROUP = 8
N_EXPERTS = 32
D_EXPERT = 256
DEPTH = 1
ALPHA = (2.0 * DEPTH) ** 0.25

LANES = 128
V7X_VMEM_LIMIT = 56 * 1024 * 1024

C_CQ = 0
C_CKV = 256
C_Z = 384
C_XBC = 896
C_KPE = 1920
C_KPER = 2048
C_DT = 2176
IN_COLS_R = 2304

TM_IN = 256
Q_BLK = 256
SSD_L = 256
TM_MID = 256
TM_MOE = 256
SCATTER_CHUNK = 2048
UNSORT_CHUNK = 1024
ROW_SUB = D_MODEL // LANES


def _cparams(sem):
    return pltpu.CompilerParams(dimension_semantics=sem, vmem_limit_bytes=V7X_VMEM_LIMIT)


def _sigmoid(x):
    return 1.0 / (1.0 + jnp.exp(-x))


def _layer_norm(x, g, b):
    mu = jnp.mean(x, axis=-1, keepdims=True)
    xc = x - mu
    var = jnp.mean(xc * xc, axis=-1, keepdims=True)
    return xc * lax.rsqrt(var + EPS) * g + b


def _fold_kernel(wqn_ref, wukt_ref, wuv_ref, woa_ref, gkv_row_ref, gkv_col_ref, wabs_ref, wof_ref, *, scale):
    for h in range(MLA_HEADS):
        wabs_ref[h] = scale * jnp.dot(wqn_ref[h], wukt_ref[h] * gkv_row_ref[...],
                                      precision=HIGHEST, preferred_element_type=F32)
        wof_ref[h] = jnp.dot(wuv_ref[h] * gkv_col_ref[...], woa_ref[h],
                             precision=HIGHEST, preferred_element_type=F32)


def _fold_weights(w_q_up, w_kv_up, mla_kv_norm, w_out, scale):
    wq = w_q_up.reshape(MLA_Q_RANK, MLA_HEADS, MLA_NOPE + MLA_ROPE)
    wkv = w_kv_up.reshape(MLA_KV_RANK, MLA_HEADS, MLA_NOPE + MLA_V)
    wqn = jnp.transpose(wq[:, :, :MLA_NOPE], (1, 0, 2))
    wukt = jnp.transpose(wkv[:, :, :MLA_NOPE], (1, 2, 0))
    wuv = jnp.transpose(wkv[:, :, MLA_NOPE:], (1, 0, 2))
    woa = w_out[:MLA_HEADS * MLA_V].reshape(MLA_HEADS, MLA_V, D_MODEL)
    wabs, wof = pl.pallas_call(
        functools.partial(_fold_kernel, scale=scale),
        out_shape=(jax.ShapeDtypeStruct((MLA_HEADS, MLA_Q_RANK, MLA_KV_RANK), F32),
                   jax.ShapeDtypeStruct((MLA_HEADS, MLA_KV_RANK, D_MODEL), F32)),
        name="fold",
    )(wqn, wukt, wuv, woa, mla_kv_norm.reshape(1, MLA_KV_RANK), mla_kv_norm.reshape(MLA_KV_RANK, 1))
    w_abs = jnp.transpose(wabs, (1, 0, 2)).reshape(MLA_Q_RANK, MLA_HEADS * MLA_KV_RANK).astype(BF16)
    w_pe3 = wq[:, :, MLA_NOPE:] * scale
    half = MLA_ROPE // 2
    w_per3 = jnp.concatenate([-w_pe3[:, :, half:], w_pe3[:, :, :half]], axis=-1)
    w_pe = w_pe3.reshape(MLA_Q_RANK, MLA_HEADS * MLA_ROPE).astype(BF16)
    w_per = w_per3.reshape(MLA_Q_RANK, MLA_HEADS * MLA_ROPE).astype(BF16)
    w_mix = jnp.concatenate([wof.reshape(MLA_HEADS * MLA_KV_RANK, D_MODEL), w_out[MLA_HEADS * MLA_V:]],
                            axis=0).astype(BF16)
    return w_abs, w_pe, w_per, w_mix


def _inproj_kernel(x_ref, w_ref, gq_ref, cw_ref, cb_ref, dtb_ref, cos_ref, sin_ref,
                   cq_ref, kc_ref, z_ref, xbc_ref, dt_ref, cbuf):
    j = pl.program_id(1)
    tm = x_ref.shape[0]
    proj = jnp.dot(x_ref[...].astype(BF16), w_ref[...], preferred_element_type=F32)

    c_q = proj[:, C_CQ:C_CQ + MLA_Q_RANK]
    cq = c_q * lax.rsqrt(jnp.mean(c_q * c_q, axis=-1, keepdims=True) + EPS) * gq_ref[...]
    cq_ref[...] = cq.astype(BF16)

    c_kv = proj[:, C_CKV:C_CKV + MLA_KV_RANK]
    ckv = c_kv * lax.rsqrt(jnp.mean(c_kv * c_kv, axis=-1, keepdims=True) + EPS)
    kpe = proj[:, C_KPE:C_KPE + LANES] * cos_ref[...] + proj[:, C_KPER:C_KPER + LANES] * sin_ref[...]
    kc_ref[...] = jnp.concatenate([ckv, kpe], axis=1).astype(BF16)

    z_ref[...] = proj[:, C_Z:C_Z + SSD_INNER].astype(BF16)

    @pl.when(j == 0)
    def _():
        cbuf[0:8, :] = jnp.zeros((8, SSD_CONV_DIM), F32)

    cbuf[8:8 + tm, :] = proj[:, C_XBC:C_XBC + SSD_CONV_DIM]
    acc = cb_ref[...] + cw_ref[SSD_CONV - 1:SSD_CONV, :] * proj[:, C_XBC:C_XBC + SSD_CONV_DIM]
    for k in range(SSD_CONV - 1):
        acc = acc + cw_ref[k:k + 1, :] * cbuf[pl.ds(8 - (SSD_CONV - 1) + k, tm), :]
    xbc_ref[...] = (acc * _sigmoid(acc)).astype(BF16)
    cbuf[0:8, :] = cbuf[tm:tm + 8, :]

    dtr = proj[:, C_DT:C_DT + LANES] + dtb_ref[...]
    dt_ref[...] = jnp.maximum(dtr, 0.0) + jnp.log(1.0 + jnp.exp(-jnp.abs(dtr)))


def _inproj(x, w_in_r, gq, conv_w, conv_b, dt_bias_p, cos4, sin4):
    b, s, d = x.shape
    tm = TM_IN
    grid = (b, s // tm)
    tok = lambda w: pl.BlockSpec((None, tm, w), lambda bi, j: (bi, j, 0))
    full = lambda a: pl.BlockSpec(a.shape, lambda bi, j: (0,) * a.ndim)
    pos = pl.BlockSpec((tm, LANES), lambda bi, j: (j, 0))
    return pl.pallas_call(
        _inproj_kernel,
        grid=grid,
        in_specs=[tok(d), full(w_in_r), full(gq), full(conv_w), full(conv_b), full(dt_bias_p), pos, pos],
        out_specs=(tok(MLA_Q_RANK), tok(2 * LANES), tok(SSD_INNER), tok(SSD_CONV_DIM), tok(LANES)),
        out_shape=(jax.ShapeDtypeStruct((b, s, MLA_Q_RANK), BF16),
                   jax.ShapeDtypeStruct((b, s, 2 * LANES), BF16),
                   jax.ShapeDtypeStruct((b, s, SSD_INNER), BF16),
                   jax.ShapeDtypeStruct((b, s, SSD_CONV_DIM), BF16),
                   jax.ShapeDtypeStruct((b, s, LANES), F32)),
        scratch_shapes=[pltpu.VMEM((tm + 8, SSD_CONV_DIM), F32)],
        compiler_params=_cparams(("arbitrary", "arbitrary")),
        name="inproj",
    )(x, w_in_r, gq, conv_w, conv_b, dt_bias_p, cos4, sin4)


def _attn_kernel(cq_ref, kc_ref, wabs_ref, wpe_ref, wper_ref, cos_ref, sin_ref, o_ref,
                 q_scr, s_scr, mx_scr, ls_scr, acc_scr):
    i = pl.program_id(1)
    qb = cq_ref.shape[0]
    rows = MLA_HEADS * qb
    cq = cq_ref[...]
    qabs = jnp.dot(cq, wabs_ref[...], preferred_element_type=F32)
    cos8 = jnp.concatenate([cos_ref[...], cos_ref[...]], axis=1)
    sin8 = jnp.concatenate([sin_ref[...], sin_ref[...]], axis=1)
    qpe = (jnp.dot(cq, wpe_ref[...], preferred_element_type=F32) * cos8
           + jnp.dot(cq, wper_ref[...], preferred_element_type=F32) * sin8)
    lane = lax.broadcasted_iota(jnp.int32, (qb, LANES), 1)
    heads_per_blk = LANES // MLA_ROPE
    for h in range(MLA_HEADS):
        pe_blk = qpe[:, LANES * (h // heads_per_blk):LANES * (h // heads_per_blk + 1)]
        pe_h = jnp.where(lane // MLA_ROPE == h % heads_per_blk, pe_blk, 0.0)
        q_scr[h * qb:(h + 1) * qb, :] = jnp.concatenate(
            [qabs[:, h * MLA_KV_RANK:(h + 1) * MLA_KV_RANK], pe_h], axis=1).astype(BF16)

    def scores(j):
        kblk = kc_ref[pl.ds(pl.multiple_of(j * qb, qb), qb), :]
        return lax.dot_general(q_scr[...], kblk, (((1,), (1,)), ((), ())), preferred_element_type=F32)

    def lane_max(s):
        return jnp.maximum(s[:, :LANES], s[:, LANES:])

    mx_scr[...] = jnp.full((rows, LANES), -jnp.inf, F32)

    def pass1(j, carry):
        s = scores(j)
        s_scr[j] = s
        mx_scr[...] = jnp.maximum(mx_scr[...], lane_max(s))
        return carry

    lax.fori_loop(0, i, pass1, 0)
    r = lax.broadcasted_iota(jnp.int32, (rows, qb), 0)
    c = lax.broadcasted_iota(jnp.int32, (rows, qb), 1)
    s = jnp.where(c // CHUNK <= (r % qb) // CHUNK, scores(i), NEG_INF)
    s_scr[i] = s
    m = jnp.max(jnp.maximum(mx_scr[...], lane_max(s)), axis=-1, keepdims=True)
    mx_scr[...] = jnp.broadcast_to(m, (rows, LANES))

    ls_scr[...] = jnp.zeros((rows, LANES), F32)
    acc_scr[...] = jnp.zeros((rows, MLA_KV_RANK), F32)

    def pass2(j, carry):
        sj = s_scr[j]
        mb = mx_scr[...]
        p = jnp.concatenate([jnp.exp2(sj[:, :LANES] - mb), jnp.exp2(sj[:, LANES:] - mb)], axis=1)
        ls_scr[...] = ls_scr[...] + p[:, :LANES] + p[:, LANES:]
        vblk = kc_ref[pl.ds(pl.multiple_of(j * qb, qb), qb), 0:MLA_KV_RANK]
        acc_scr[...] = acc_scr[...] + jnp.dot(p.astype(BF16), vblk, preferred_element_type=F32)
        return carry

    lax.fori_loop(0, i + 1, pass2, 0)

    o = acc_scr[...] * (1.0 / jnp.sum(ls_scr[...], axis=-1, keepdims=True))
    for h in range(MLA_HEADS):
        o_ref[:, h * MLA_KV_RANK:(h + 1) * MLA_KV_RANK] = o[h * qb:(h + 1) * qb, :].astype(BF16)


def _attention(cq, kc, w_abs, w_pe, w_per, cos4, sin4):
    b, s, _ = cq.shape
    qb = Q_BLK
    rows = MLA_HEADS * qb
    full = lambda a: pl.BlockSpec(a.shape, lambda bi, i: (0,) * a.ndim)
    pos = pl.BlockSpec((qb, LANES), lambda bi, i: (i, 0))
    return pl.pallas_call(
        _attn_kernel,
        grid=(b, s // qb),
        in_specs=[pl.BlockSpec((None, qb, MLA_Q_RANK), lambda bi, i: (bi, i, 0)),
                  pl.BlockSpec((None, s, 2 * LANES), lambda bi, i: (bi, 0, 0)),
                  full(w_abs), full(w_pe), full(w_per), pos, pos],
        out_specs=pl.BlockSpec((None, qb, MLA_HEADS * MLA_KV_RANK), lambda bi, i: (bi, i, 0)),
        out_shape=jax.ShapeDtypeStruct((b, s, MLA_HEADS * MLA_KV_RANK), BF16),
        scratch_shapes=[pltpu.VMEM((rows, 2 * LANES), BF16),
                        pltpu.VMEM((s // qb, rows, qb), F32),
                        pltpu.VMEM((rows, LANES), F32),
                        pltpu.VMEM((rows, LANES), F32),
                        pltpu.VMEM((rows, MLA_KV_RANK), F32)],
        compiler_params=_cparams(("arbitrary", "arbitrary")),
        name="attn",
    )(cq, kc, w_abs, w_pe, w_per, cos4, sin4)


def _ssd_kernel(xbc_ref, dt_ref, z_ref, alog_ref, dfull_ref, norm_ref, e_ref, o_ref, state_scr):
    c = pl.program_id(1)
    ln = xbc_ref.shape[0]
    gw = SSD_INNER // SSD_GROUPS
    hpg = SSD_HEADS // SSD_GROUPS

    @pl.when(c == 0)
    def _():
        state_scr[...] = jnp.zeros(state_scr.shape, F32)

    xs = xbc_ref[:, 0:SSD_INNER].astype(F32)
    dt = dt_ref[...]
    lane1 = lax.broadcasted_iota(jnp.int32, (1, LANES), 1)
    a = jnp.where(lane1 < SSD_HEADS, -jnp.exp(alog_ref[...]), 0.0)
    adt = dt * a
    row = lax.broadcasted_iota(jnp.int32, (ln, ln), 0)
    col = lax.broadcasted_iota(jnp.int32, (ln, ln), 1)
    causal = col <= row
    acs = jnp.dot(causal.astype(F32), adt, precision=HIGHEST, preferred_element_type=F32)
    e = e_ref[...]
    acs_e = jnp.dot(acs, e, precision=HIGHEST, preferred_element_type=F32)
    dt_e = jnp.dot(dt, e, precision=HIGHEST, preferred_element_type=F32)
    acs_end = acs_e[ln - 1:ln, :]
    xdt = xs * dt_e
    x_end = (xdt * jnp.exp(acs_end - acs_e)).astype(BF16)
    eacs = jnp.exp(acs_e)
    chunk_decay = jnp.exp(acs_end)
    acs_t = acs.T
    lane_g = lax.broadcasted_iota(jnp.int32, (ln, gw), 1)

    ys = []
    for g in range(SSD_GROUPS):
        bg = xbc_ref[:, SSD_INNER + g * SSD_STATE:SSD_INNER + (g + 1) * SSD_STATE]
        cg = xbc_ref[:, SSD_INNER + SSD_GROUPS * SSD_STATE + g * SSD_STATE:
                     SSD_INNER + SSD_GROUPS * SSD_STATE + (g + 1) * SSD_STATE]
        cb = lax.dot_general(cg, bg, (((1,), (1,)), ((), ())), preferred_element_type=F32)
        prev = state_scr[g]
        y = jnp.dot(cg, prev.astype(BF16), preferred_element_type=F32) * eacs[:, g * gw:(g + 1) * gw]
        xg = xdt[:, g * gw:(g + 1) * gw]
        for hh in range(hpg):
            h = g * hpg + hh
            seg = acs[:, h:h + 1] - acs_t[h:h + 1, :]
            dec = jnp.exp(jnp.where(causal, seg, -jnp.inf))
            xm = jnp.where(lane_g // SSD_HEADDIM == hh, xg, 0.0).astype(BF16)
            y = y + jnp.dot((cb * dec).astype(BF16), xm, preferred_element_type=F32)
        st = lax.dot_general(bg, x_end[:, g * gw:(g + 1) * gw], (((0,), (0,)), ((), ())),
                             preferred_element_type=F32)
        state_scr[g] = chunk_decay[:, g * gw:(g + 1) * gw] * prev + st
        ys.append(y)

    y = jnp.concatenate(ys, axis=1) + dfull_ref[...] * xs
    zf = z_ref[...].astype(F32)
    yz = y * (zf * _sigmoid(zf))
    outs = []
    for g in range(SSD_GROUPS):
        yg = yz[:, g * gw:(g + 1) * gw]
        outs.append(yg * lax.rsqrt(jnp.mean(yg * yg, axis=-1, keepdims=True) + EPS))
    o_ref[...] = (jnp.concatenate(outs, axis=1) * norm_ref[...]).astype(BF16)


def _ssd(xbc, dt, z, alog_p, d_full, ssd_norm, expand):
    b, s, _ = xbc.shape
    ln = SSD_L
    tok = lambda w: pl.BlockSpec((None, ln, w), lambda bi, c: (bi, c, 0))
    full = lambda a: pl.BlockSpec(a.shape, lambda bi, c: (0,) * a.ndim)
    return pl.pallas_call(
        _ssd_kernel,
        grid=(b, s // ln),
        in_specs=[tok(SSD_CONV_DIM), tok(LANES), tok(SSD_INNER), full(alog_p), full(d_full), full(ssd_norm),
                  full(expand)],
        out_specs=tok(SSD_INNER),
        out_shape=jax.ShapeDtypeStruct((b, s, SSD_INNER), BF16),
        scratch_shapes=[pltpu.VMEM((SSD_GROUPS, SSD_STATE, SSD_INNER // SSD_GROUPS), F32)],
        compiler_params=_cparams(("arbitrary", "arbitrary")),
        name="ssd",
    )(xbc, dt, z, alog_p, d_full, ssd_norm, expand)


def _memkv_kernel(mem_ref, wk_ref, wv_ref, k_ref, v_ref):
    m = mem_ref[...].astype(BF16)
    k_ref[...] = jnp.dot(m, wk_ref[...], preferred_element_type=F32).astype(BF16)
    v_ref[...] = jnp.dot(m, wv_ref[...], preferred_element_type=F32).astype(BF16)


def _memkv(mem, wk, wv):
    b, m, d = mem.shape
    blk = pl.BlockSpec((None, m, d), lambda bi: (bi, 0, 0))
    full = lambda a: pl.BlockSpec(a.shape, lambda bi: (0,) * a.ndim)
    return pl.pallas_call(
        _memkv_kernel,
        grid=(b,),
        in_specs=[blk, full(wk), full(wv)],
        out_specs=(blk, blk),
        out_shape=(jax.ShapeDtypeStruct((b, m, d), BF16), jax.ShapeDtypeStruct((b, m, d), BF16)),
        compiler_params=_cparams(("arbitrary",)),
        name="memkv",
    )(mem, wk, wv)


def _split_bf16(w):
    hi = w.astype(BF16)
    return hi, (w - hi.astype(F32)).astype(BF16)


def _router_logits(h, wrh_ref, wrl_ref, br_ref):
    h_hi, h_lo = _split_bf16(h)
    return (jnp.dot(h_hi, wrh_ref[...], preferred_element_type=F32)
            + jnp.dot(h_hi, wrl_ref[...], preferred_element_type=F32)
            + jnp.dot(h_lo, wrh_ref[...], preferred_element_type=F32)) + br_ref[...]


def _is_group_lane(lane):
    return (lane >= N_EXPERTS) & (lane < N_EXPERTS + N_EXPERT_GROUPS)


def _mid_kernel(x_ref, ol_ref, ss_ref, wmix_ref, g1_ref, b1_ref, wq_ref, km_ref, vm_ref, wo_ref,
                g2_ref, b2_ref, wrh_ref, wrl_ref, br_ref, row_ref, meta_ref, cnt_ref, run_scr):
    step = pl.program_id(0) * pl.num_programs(1) + pl.program_id(1)
    tm = x_ref.shape[0]
    n_lat = MLA_HEADS * MLA_KV_RANK

    @pl.when(step == 0)
    def _():
        run_scr[...] = jnp.zeros(run_scr.shape, F32)

    mix = (jnp.dot(ol_ref[...], wmix_ref[0:n_lat, :], preferred_element_type=F32)
           + jnp.dot(ss_ref[...], wmix_ref[n_lat:, :], preferred_element_type=F32))
    h1 = _layer_norm(ALPHA * x_ref[...] + mix, g1_ref[...], b1_ref[...])

    q = jnp.dot(h1.astype(BF16), wq_ref[...], preferred_element_type=F32).astype(BF16)
    outs = []
    for h in range(XA_HEADS):
        sl = slice(h * XA_HEAD_DIM, (h + 1) * XA_HEAD_DIM)
        s = lax.dot_general(q[:, sl], km_ref[:, sl], (((1,), (1,)), ((), ())), preferred_element_type=F32)
        p = jnp.exp(s - jnp.max(s, axis=-1, keepdims=True))
        o = jnp.dot(p.astype(BF16), vm_ref[:, sl], preferred_element_type=F32)
        outs.append((o / jnp.sum(p, axis=-1, keepdims=True)).astype(BF16))
    xa = jnp.dot(jnp.concatenate(outs, axis=1), wo_ref[...], preferred_element_type=F32)
    h2 = _layer_norm(ALPHA * h1 + xa, g2_ref[...], b2_ref[...])

    logits = _router_logits(h2, wrh_ref, wrl_ref, br_ref)
    lane = lax.broadcasted_iota(jnp.int32, (tm, LANES), 1)
    is_g = _is_group_lane(lane)
    gmax = jnp.max(jnp.where(is_g, logits, -jnp.inf), axis=-1, keepdims=True)
    g_idx = jnp.min(jnp.where(is_g & (logits == gmax), lane - N_EXPERTS, LANES), axis=-1, keepdims=True)
    onehot = (lane == g_idx).astype(BF16)
    r = lax.broadcasted_iota(jnp.int32, (tm, tm), 0)
    c = lax.broadcasted_iota(jnp.int32, (tm, tm), 1)
    before = jnp.dot((c < r).astype(BF16), onehot, preferred_element_type=F32) + run_scr[...]
    rank = jnp.sum(jnp.where(lane == g_idx, before, 0.0), axis=-1, keepdims=True)
    run_scr[...] = run_scr[...] + jnp.sum(onehot.astype(F32), axis=0, keepdims=True)
    cnt_ref[...] = run_scr[...]
    meta_ref[...] = jnp.where(lane == 0, g_idx.astype(F32), jnp.where(lane == 1, rank, 0.0))

    for k in range(ROW_SUB):
        row_ref[:, k, :] = h2[:, k * LANES:(k + 1) * LANES]


def _mid(x, o_lat, ssd_out, w_mix, g1, b1, wq, kmem, vmem, wo, g2, b2, wrh, wrl, br):
    b, s, d = x.shape
    tm = TM_MID
    tok = lambda w: pl.BlockSpec((None, tm, w), lambda bi, j: (bi, j, 0))
    full = lambda a: pl.BlockSpec(a.shape, lambda bi, j: (0,) * a.ndim)
    mem = pl.BlockSpec((None,) + kmem.shape[1:], lambda bi, j: (bi, 0, 0))
    return pl.pallas_call(
        _mid_kernel,
        grid=(b, s // tm),
        in_specs=[tok(d), tok(o_lat.shape[-1]), tok(SSD_INNER), full(w_mix), full(g1), full(b1), full(wq),
                  mem, mem, full(wo), full(g2), full(b2), full(wrh), full(wrl), full(br)],
        out_specs=(pl.BlockSpec((None, tm, ROW_SUB, LANES), lambda bi, j: (bi, j, 0, 0)), tok(LANES),
                   pl.BlockSpec((1, LANES), lambda bi, j: (0, 0))),
        out_shape=(jax.ShapeDtypeStruct((b, s, ROW_SUB, LANES), F32),
                   jax.ShapeDtypeStruct((b, s, LANES), F32),
                   jax.ShapeDtypeStruct((1, LANES), F32)),
        scratch_shapes=[pltpu.VMEM((1, LANES), F32)],
        compiler_params=_cparams(("arbitrary", "arbitrary")),
        name="mid",
    )(x, o_lat, ssd_out, w_mix, g1, b1, wq, kmem, vmem, wo, g2, b2, wrh, wrl, br)


def _scatter_kernel(dest_ref, padrow_ref, rows_hbm, out_hbm, zero_scr, sem, psem):
    c = pl.program_id(0)
    n = dest_ref.shape[1]
    base = c * n

    def issue(t, carry):
        pltpu.make_async_copy(rows_hbm.at[base + t], out_hbm.at[dest_ref[0, t]], sem).start()
        return carry

    lax.fori_loop(0, n, issue, 0)

    npad = padrow_ref.shape[1]

    @pl.when(c == 0)
    def _():
        zero_scr[...] = jnp.zeros(zero_scr.shape, F32)

        def issue_pad(k, carry):
            pltpu.make_async_copy(zero_scr, out_hbm.at[padrow_ref[0, k]], psem).start()
            return carry

        lax.fori_loop(0, npad, issue_pad, 0)

        def wait_pad(k, carry):
            pltpu.make_async_copy(zero_scr, out_hbm.at[0], psem).wait()
            return carry

        lax.fori_loop(0, npad, wait_pad, 0)

    pltpu.make_async_copy(rows_hbm.at[pl.ds(0, n)], out_hbm.at[pl.ds(0, n)], sem).wait()


def _scatter_rows(rows, dest3, padrow, n_out):
    t = rows.shape[0]
    n = dest3.shape[-1]
    return pl.pallas_call(
        _scatter_kernel,
        grid=(t // n,),
        in_specs=[pl.BlockSpec((None, 1, n), lambda c: (c, 0, 0), memory_space=pltpu.SMEM),
                  pl.BlockSpec(padrow.shape, lambda c: (0, 0), memory_space=pltpu.SMEM),
                  pl.BlockSpec(memory_space=pl.ANY)],
        out_specs=pl.BlockSpec(memory_space=pl.ANY),
        out_shape=jax.ShapeDtypeStruct((n_out,) + rows.shape[1:], rows.dtype),
        scratch_shapes=[pltpu.VMEM(rows.shape[1:], F32), pltpu.SemaphoreType.DMA(()),
                        pltpu.SemaphoreType.DMA(())],
        compiler_params=_cparams(("arbitrary",)),
        name="scatter",
    )(dest3, padrow, rows)


def _unsort_kernel(dcur_ref, dnext_ref, rows_hbm, o_ref, buf, sem):
    c = pl.program_id(0)
    n = o_ref.shape[0]
    slot = c % 2

    def issue(dref, sl):
        def body(t, carry):
            pltpu.make_async_copy(rows_hbm.at[dref[0, t]], buf.at[sl, t], sem.at[sl]).start()
            return carry

        lax.fori_loop(0, n, body, 0)

    @pl.when(c == 0)
    def _():
        issue(dcur_ref, 0)

    @pl.when(c + 1 < pl.num_programs(0))
    def _():
        issue(dnext_ref, 1 - slot)

    pltpu.make_async_copy(rows_hbm.at[pl.ds(0, n)], buf.at[slot], sem.at[slot]).wait()
    for k in range(ROW_SUB):
        o_ref[:, k * LANES:(k + 1) * LANES] = buf[slot, :, k, :]


def _unsort_rows(rows, dest3, n_tok):
    n = dest3.shape[-1]
    nc = n_tok // n
    return pl.pallas_call(
        _unsort_kernel,
        grid=(nc,),
        in_specs=[pl.BlockSpec((None, 1, n), lambda c: (c, 0, 0), memory_space=pltpu.SMEM),
                  pl.BlockSpec((None, 1, n), lambda c: (jnp.minimum(c + 1, nc - 1), 0, 0),
                               memory_space=pltpu.SMEM),
                  pl.BlockSpec(memory_space=pl.ANY)],
        out_specs=pl.BlockSpec((n, D_MODEL), lambda c: (c, 0)),
        out_shape=jax.ShapeDtypeStruct((n_tok, D_MODEL), rows.dtype),
        scratch_shapes=[pltpu.VMEM((2, n) + rows.shape[1:], F32), pltpu.SemaphoreType.DMA((2,))],
        compiler_params=_cparams(("arbitrary",)),
        name="unsort",
    )(dest3, dest3, rows)


def _moe_kernel(gid_ref, nused_ref, rows_ref, wrh_ref, wrl_ref, br_ref, wg_ref, wu_ref, wd_ref, g3_ref, b3_ref,
                o_ref):
    i = pl.program_id(0)
    tm = rows_ref.shape[0]

    @pl.when(i < nused_ref[0])
    def _():
        g = gid_ref[i]
        x = jnp.concatenate([rows_ref[:, k, :] for k in range(ROW_SUB)], axis=1)

        logits = _router_logits(x, wrh_ref, wrl_ref, br_ref)
        lane = lax.broadcasted_iota(jnp.int32, (tm, LANES), 1)
        is_g = _is_group_lane(lane)
        gmax = jnp.max(jnp.where(is_g, logits, -jnp.inf), axis=-1, keepdims=True)
        gsel = jnp.sum(jnp.where(lane == N_EXPERTS + g, logits, 0.0), axis=-1, keepdims=True)
        g_gate = jnp.exp(gsel - gmax) / jnp.sum(jnp.where(is_g, jnp.exp(logits - gmax), 0.0),
                                                axis=-1, keepdims=True)
        in_grp = (lane < N_EXPERTS) & (lane // EXPERTS_PER_GROUP == g)
        t1 = jnp.max(jnp.where(in_grp, logits, -jnp.inf), axis=-1, keepdims=True)
        i1 = jnp.min(jnp.where(in_grp & (logits == t1), lane, LANES), axis=-1, keepdims=True)
        rest = in_grp & (lane != i1)
        t2 = jnp.max(jnp.where(rest, logits, -jnp.inf), axis=-1, keepdims=True)
        i2 = jnp.min(jnp.where(rest & (logits == t2), lane, LANES), axis=-1, keepdims=True)
        e21 = jnp.exp(t2 - t1)
        w1 = g_gate / (1.0 + e21)
        comb = jnp.where(lane == i1, w1, 0.0) + jnp.where(lane == i2, w1 * e21, 0.0)

        xb = x.astype(BF16)
        acc = jnp.zeros(x.shape, F32)
        for e in range(EXPERTS_PER_GROUP):
            ce = jnp.sum(jnp.where(lane == g * EXPERTS_PER_GROUP + e, comb, 0.0), axis=-1, keepdims=True)
            hg = jnp.dot(xb, wg_ref[e], preferred_element_type=F32)
            hu = jnp.dot(xb, wu_ref[e], preferred_element_type=F32)
            hd = hg * _sigmoid(hg) * hu * ce
            acc = acc + jnp.dot(hd.astype(BF16), wd_ref[e], preferred_element_type=F32)
        y = _layer_norm(ALPHA * x + acc, g3_ref[...], b3_ref[...])
        for k in range(ROW_SUB):
            o_ref[:, k, :] = y[:, k * LANES:(k + 1) * LANES]

    @pl.when(i >= nused_ref[0])
    def _():
        o_ref[...] = jnp.zeros(o_ref.shape, F32)


def _moe(rows_sorted, tile_gid, n_used, wrh, wrl, br, wg, wu, wd, g3, b3):
    n_rows = rows_sorted.shape[0]
    tm = TM_MOE
    nt = n_rows // tm

    def tile_map(i, gid, nused):
        return (jnp.minimum(i, nused[0] - 1), 0, 0)

    def w_map(i, gid, nused):
        return (gid[i], 0, 0)

    full = lambda a: pl.BlockSpec(a.shape, lambda i, gid, nused: (0,) * a.ndim)
    e = EXPERTS_PER_GROUP
    grid_spec = pltpu.PrefetchScalarGridSpec(
        num_scalar_prefetch=2,
        grid=(nt,),
        in_specs=[pl.BlockSpec((tm, ROW_SUB, LANES), tile_map), full(wrh), full(wrl), full(br),
                  pl.BlockSpec((e, D_MODEL, D_EXPERT), w_map),
                  pl.BlockSpec((e, D_MODEL, D_EXPERT), w_map),
                  pl.BlockSpec((e, D_EXPERT, D_MODEL), w_map),
                  full(g3), full(b3)],
        out_specs=pl.BlockSpec((tm, ROW_SUB, LANES), lambda i, gid, nused: (i, 0, 0)),
    )
    return pl.pallas_call(
        _moe_kernel,
        grid_spec=grid_spec,
        out_shape=jax.ShapeDtypeStruct((n_rows, ROW_SUB, LANES), F32),
        compiler_params=_cparams(("arbitrary",)),
        name="moe",
    )(tile_gid, n_used, rows_sorted, wrh, wrl, br, wg, wu, wd, g3, b3)


def _rope_tables(seq):
    pos = jnp.arange(seq, dtype=F32)
    inv_freq = ROPE_THETA ** (-jnp.arange(0, MLA_ROPE, 2, dtype=F32) / MLA_ROPE)
    ang = pos[:, None] * inv_freq[None, :]
    cos = jnp.tile(jnp.cos(ang), (1, LANES // (MLA_ROPE // 2)))
    sin = jnp.tile(jnp.sin(ang), (1, LANES // (MLA_ROPE // 2)))
    return cos, sin


def _pad_lanes(v, n=LANES):
    v = v.reshape(1, -1)
    return jnp.pad(v, ((0, 0), (0, n - v.shape[1])))


def kernel(x, mem, w_in, mla_q_norm, w_q_up, mla_kv_norm, w_kv_up, ssd_conv_w, ssd_conv_b, ssd_dt_bias, ssd_a_log, ssd_d, ssd_norm, w_out, ln1_g, ln1_b, xa_wq, xa_wk, xa_wv, xa_wo, ln2_g, ln2_b, router_group_w, router_group_b, router_expert_w, router_expert_b, expert_w_gate, expert_w_up, expert_w_down, ln3_g, ln3_b):
    b, s, d = x.shape
    t = b * s
    l = 0
    row = lambda v: v[l].reshape(1, -1)

    wi = w_in[l]
    o1 = MLA_Q_RANK
    o2 = o1 + MLA_KV_RANK
    o3 = o2 + MLA_ROPE
    o4 = o3 + SSD_INNER
    o5 = o4 + SSD_CONV_DIM
    w_kr = wi[:, o2:o3]
    half = MLA_ROPE // 2
    w_kr_rot = jnp.concatenate([-w_kr[:, half:], w_kr[:, :half]], axis=1)
    reps = LANES // MLA_ROPE
    w_in_r = jnp.concatenate(
        [wi[:, :o2], wi[:, o3:o4], wi[:, o4:o5], jnp.tile(w_kr, (1, reps)), jnp.tile(w_kr_rot, (1, reps)),
         wi[:, o5:], jnp.zeros((d, LANES - SSD_HEADS), F32)], axis=1).astype(BF16)
    cos4, sin4 = _rope_tables(s)
    attn_scale = (MLA_NOPE + MLA_ROPE) ** -0.5 * math.log2(math.e)
    w_abs, w_pe, w_per, w_mix = _fold_weights(w_q_up[l], w_kv_up[l], mla_kv_norm[l], w_out[l], attn_scale)
    expand = (jnp.arange(LANES)[:, None] == (jnp.arange(SSD_INNER)[None, :] // SSD_HEADDIM)).astype(F32)
    d_full = jnp.repeat(ssd_d[l], SSD_HEADDIM).reshape(1, SSD_INNER)
    wrh, wrl = _split_bf16(jnp.concatenate(
        [router_expert_w[l], router_group_w[l], jnp.zeros((d, LANES - N_EXPERTS - N_EXPERT_GROUPS), F32)], axis=1))
    br = _pad_lanes(jnp.concatenate([router_expert_b[l], router_group_b[l]]))

    cq, kc, z, xbc, dt = _inproj(x, w_in_r, row(mla_q_norm), ssd_conv_w[l], row(ssd_conv_b),
                                 _pad_lanes(ssd_dt_bias[l]), cos4, sin4)
    o_lat = _attention(cq, kc, w_abs, w_pe, w_per, cos4, sin4)
    ssd_out = _ssd(xbc, dt, z, _pad_lanes(ssd_a_log[l]), d_full, row(ssd_norm), expand)

    kmem, vmem = _memkv(mem, xa_wk[l].astype(BF16), xa_wv[l].astype(BF16))
    wq = (xa_wq[l] * (XA_HEAD_DIM ** -0.5)).astype(BF16)
    rows, meta, counts = _mid(x, o_lat, ssd_out, w_mix, row(ln1_g), row(ln1_b), wq, kmem, vmem,
                              xa_wo[l].astype(BF16), row(ln2_g), row(ln2_b), wrh, wrl, br)

    tm = TM_MOE
    nt = t // tm + N_EXPERT_GROUPS - 1
    meta = meta.reshape(t, LANES)
    g_idx = meta[:, 0].astype(jnp.int32)
    rank = meta[:, 1].astype(jnp.int32)
    cnt = counts[0, :N_EXPERT_GROUPS].astype(jnp.int32)
    ntile = (cnt + tm - 1) // tm
    tile_end = jnp.cumsum(ntile)
    tile_start = tile_end - ntile
    dest = tile_start[g_idx] * tm + rank
    n_used = tile_end[-1:].astype(jnp.int32)
    tile_gid = jnp.minimum(jnp.sum(jnp.arange(nt)[:, None] >= tile_end[None, :], axis=1),
                           N_EXPERT_GROUPS - 1).astype(jnp.int32)
    pad_end = jnp.cumsum(ntile * tm - cnt)
    k = jnp.arange(nt * tm - t)
    kg = jnp.sum(k[:, None] >= pad_end[None, :], axis=1)
    kgc = jnp.minimum(kg, N_EXPERT_GROUPS - 1)
    off = k - jnp.where(kg > 0, pad_end[jnp.maximum(kg, 1) - 1], 0)
    padrow = jnp.where(kg < N_EXPERT_GROUPS, tile_start[kgc] * tm + cnt[kgc] + off,
                       tile_end[-1] * tm + off).astype(jnp.int32).reshape(1, -1)
    sc_chunk = min(SCATTER_CHUNK, t)
    un_chunk = min(UNSORT_CHUNK, t)

    rows_sorted = _scatter_rows(rows.reshape(t, ROW_SUB, LANES), dest.reshape(t // sc_chunk, 1, sc_chunk),
                                padrow, nt * tm)
    out_sorted = _moe(rows_sorted, tile_gid, n_used, wrh, wrl, br, expert_w_gate[l].astype(BF16),
                      expert_w_up[l].astype(BF16), expert_w_down[l].astype(BF16), row(ln3_g), row(ln3_b))
    out = _unsort_rows(out_sorted, dest.reshape(t // un_chunk, 1, un_chunk), t)
    return out.reshape(b, s, d)
```

```python
import functools
import math

import jax
import jax.numpy as jnp
from jax import lax
from jax.experimental import pallas as pl
from jax.experimental.pallas import tpu as pltpu

F32 = jnp.float32
BF16 = jnp.bfloat16
HIGHEST = lax.Precision.HIGHEST

EPS = 1e-5
NEG_INF = -1e30
CHUNK = 64
ROPE_THETA = 10000.0

D_MODEL = 1024
MLA_HEADS = 8
MLA_NOPE = 64
MLA_ROPE = 32
MLA_V = 64
MLA_Q_RANK = 256
MLA_KV_RANK = 128
SSD_HEADS = 8
SSD_HEADDIM = 64
SSD_INNER = 512
SSD_GROUPS = 2
SSD_STATE = 128
SSD_CONV = 4
SSD_CONV_DIM = 1024
XA_HEADS = 4
XA_HEAD_DIM = 256
N_EXPERT_GROUPS = 4
EXPERTS_PER_GROUP = 8
N_EXPERTS = 32
D_EXPERT = 256
DEPTH = 1
ALPHA = (2.0 * DEPTH) ** 0.25

LANES = 128
V7X_VMEM_LIMIT = 56 * 1024 * 1024

C_CQ = 0
C_CKV = 256
C_Z = 384
C_XBC = 896
C_KPE = 1920
C_KPER = 2048
C_DT = 2176
IN_COLS_R = 2304

TM_IN = 256
Q_BLK = 256
SSD_L = 256
TM_MID = 256
TM_MOE = 256
SCATTER_CHUNK = 2048
UNSORT_CHUNK = 1024
ROW_SUB = D_MODEL // LANES


def _cparams(sem):
    return pltpu.CompilerParams(dimension_semantics=sem, vmem_limit_bytes=V7X_VMEM_LIMIT)


def _sigmoid(x):
    return 1.0 / (1.0 + jnp.exp(-x))


def _split3(x):
    hi = x.astype(BF16)
    r = x - hi.astype(F32)
    mid = r.astype(BF16)
    return hi, mid, (r - mid.astype(F32)).astype(BF16)


def _row(t):
    return pl.ds(pl.multiple_of(t * ROW_SUB, ROW_SUB), ROW_SUB)


def _lane_block(k, n):
    return pl.ds(k, n, stride=ROW_SUB)


def _layer_norm(x, g, b):
    mu = jnp.mean(x, axis=-1, keepdims=True)
    xc = x - mu
    var = jnp.mean(xc * xc, axis=-1, keepdims=True)
    return xc * lax.rsqrt(var + EPS) * g + b


def _fold_kernel(wqn_ref, wukt_ref, wuv_ref, woa_ref, gkv_row_ref, gkv_col_ref, wabs_ref, wof_ref, *, scale):
    for h in range(MLA_HEADS):
        wabs_ref[h] = scale * jnp.dot(wqn_ref[h], wukt_ref[h] * gkv_row_ref[...],
                                      precision=HIGHEST, preferred_element_type=F32)
        wof_ref[h] = jnp.dot(wuv_ref[h] * gkv_col_ref[...], woa_ref[h],
                             precision=HIGHEST, preferred_element_type=F32)


def _fold_weights(w_q_up, w_kv_up, mla_kv_norm, w_out, scale):
    wq = w_q_up.reshape(MLA_Q_RANK, MLA_HEADS, MLA_NOPE + MLA_ROPE)
    wkv = w_kv_up.reshape(MLA_KV_RANK, MLA_HEADS, MLA_NOPE + MLA_V)
    wqn = jnp.transpose(wq[:, :, :MLA_NOPE], (1, 0, 2))
    wukt = jnp.transpose(wkv[:, :, :MLA_NOPE], (1, 2, 0))
    wuv = jnp.transpose(wkv[:, :, MLA_NOPE:], (1, 0, 2))
    woa = w_out[:MLA_HEADS * MLA_V].reshape(MLA_HEADS, MLA_V, D_MODEL)
    wabs, wof = pl.pallas_call(
        functools.partial(_fold_kernel, scale=scale),
        out_shape=(jax.ShapeDtypeStruct((MLA_HEADS, MLA_Q_RANK, MLA_KV_RANK), F32),
                   jax.ShapeDtypeStruct((MLA_HEADS, MLA_KV_RANK, D_MODEL), F32)),
        name="fold",
    )(wqn, wukt, wuv, woa, mla_kv_norm.reshape(1, MLA_KV_RANK), mla_kv_norm.reshape(MLA_KV_RANK, 1))
    w_abs = jnp.transpose(wabs, (1, 0, 2)).reshape(MLA_Q_RANK, MLA_HEADS * MLA_KV_RANK).astype(BF16)
    w_pe3 = wq[:, :, MLA_NOPE:] * scale
    half = MLA_ROPE // 2
    w_per3 = jnp.concatenate([-w_pe3[:, :, half:], w_pe3[:, :, :half]], axis=-1)
    w_pe = w_pe3.reshape(MLA_Q_RANK, MLA_HEADS * MLA_ROPE).astype(BF16)
    w_per = w_per3.reshape(MLA_Q_RANK, MLA_HEADS * MLA_ROPE).astype(BF16)
    w_mix = jnp.concatenate([wof.reshape(MLA_HEADS * MLA_KV_RANK, D_MODEL), w_out[MLA_HEADS * MLA_V:]],
                            axis=0).astype(BF16)
    return w_abs, w_pe, w_per, w_mix


def _inproj_kernel(x_ref, w_ref, gq_ref, cw_ref, cb_ref, dtb_ref, cos_ref, sin_ref,
                   cq_ref, kc_ref, z_ref, xbc_ref, dt_ref, cbuf):
    j = pl.program_id(1)
    tm = x_ref.shape[0]
    proj = jnp.dot(x_ref[...].astype(BF16), w_ref[...], preferred_element_type=F32)

    c_q = proj[:, C_CQ:C_CQ + MLA_Q_RANK]
    cq = c_q * lax.rsqrt(jnp.mean(c_q * c_q, axis=-1, keepdims=True) + EPS) * gq_ref[...]
    cq_ref[...] = cq.astype(BF16)

    c_kv = proj[:, C_CKV:C_CKV + MLA_KV_RANK]
    ckv = c_kv * lax.rsqrt(jnp.mean(c_kv * c_kv, axis=-1, keepdims=True) + EPS)
    kpe = proj[:, C_KPE:C_KPE + LANES] * cos_ref[...] + proj[:, C_KPER:C_KPER + LANES] * sin_ref[...]
    kc_ref[...] = jnp.concatenate([ckv, kpe], axis=1).astype(BF16)

    z_ref[...] = proj[:, C_Z:C_Z + SSD_INNER].astype(BF16)

    @pl.when(j == 0)
    def _():
        cbuf[0:8, :] = jnp.zeros((8, SSD_CONV_DIM), F32)

    cbuf[8:8 + tm, :] = proj[:, C_XBC:C_XBC + SSD_CONV_DIM]
    acc = cb_ref[...] + cw_ref[SSD_CONV - 1:SSD_CONV, :] * proj[:, C_XBC:C_XBC + SSD_CONV_DIM]
    for k in range(SSD_CONV - 1):
        acc = acc + cw_ref[k:k + 1, :] * cbuf[pl.ds(8 - (SSD_CONV - 1) + k, tm), :]
    xbc_ref[...] = (acc * _sigmoid(acc)).astype(BF16)
    cbuf[0:8, :] = cbuf[tm:tm + 8, :]

    dtr = proj[:, C_DT:C_DT + LANES] + dtb_ref[...]
    dt_ref[...] = jnp.maximum(dtr, 0.0) + jnp.log(1.0 + jnp.exp(-jnp.abs(dtr)))


def _inproj(x, w_in_r, gq, conv_w, conv_b, dt_bias_p, cos4, sin4):
    b, s, d = x.shape
    tm = TM_IN
    grid = (b, s // tm)
    tok = lambda w: pl.BlockSpec((None, tm, w), lambda bi, j: (bi, j, 0))
    full = lambda a: pl.BlockSpec(a.shape, lambda bi, j: (0,) * a.ndim)
    pos = pl.BlockSpec((tm, LANES), lambda bi, j: (j, 0))
    return pl.pallas_call(
        _inproj_kernel,
        grid=grid,
        in_specs=[tok(d), full(w_in_r), full(gq), full(conv_w), full(conv_b), full(dt_bias_p), pos, pos],
        out_specs=(tok(MLA_Q_RANK), tok(2 * LANES), tok(SSD_INNER), tok(SSD_CONV_DIM), tok(LANES)),
        out_shape=(jax.ShapeDtypeStruct((b, s, MLA_Q_RANK), BF16),
                   jax.ShapeDtypeStruct((b, s, 2 * LANES), BF16),
                   jax.ShapeDtypeStruct((b, s, SSD_INNER), BF16),
                   jax.ShapeDtypeStruct((b, s, SSD_CONV_DIM), BF16),
                   jax.ShapeDtypeStruct((b, s, LANES), F32)),
        scratch_shapes=[pltpu.VMEM((tm + 8, SSD_CONV_DIM), F32)],
        compiler_params=_cparams(("arbitrary", "arbitrary")),
        name="inproj",
    )(x, w_in_r, gq, conv_w, conv_b, dt_bias_p, cos4, sin4)


def _attn_kernel(cq_ref, kc_ref, wabs_ref, wpe_ref, wper_ref, cos_ref, sin_ref, o_ref,
                 q_scr, s_scr, mx_scr, ls_scr, acc_scr):
    i = pl.program_id(1)
    qb = cq_ref.shape[0]
    rows = MLA_HEADS * qb
    cq = cq_ref[...]
    qabs = jnp.dot(cq, wabs_ref[...], preferred_element_type=F32)
    cos8 = jnp.concatenate([cos_ref[...], cos_ref[...]], axis=1)
    sin8 = jnp.concatenate([sin_ref[...], sin_ref[...]], axis=1)
    qpe = (jnp.dot(cq, wpe_ref[...], preferred_element_type=F32) * cos8
           + jnp.dot(cq, wper_ref[...], preferred_element_type=F32) * sin8)
    lane = lax.broadcasted_iota(jnp.int32, (qb, LANES), 1)
    heads_per_blk = LANES // MLA_ROPE
    for h in range(MLA_HEADS):
        pe_blk = qpe[:, LANES * (h // heads_per_blk):LANES * (h // heads_per_blk + 1)]
        pe_h = jnp.where(lane // MLA_ROPE == h % heads_per_blk, pe_blk, 0.0)
        q_scr[h * qb:(h + 1) * qb, :] = jnp.concatenate(
            [qabs[:, h * MLA_KV_RANK:(h + 1) * MLA_KV_RANK], pe_h], axis=1).astype(BF16)

    def scores(j):
        kblk = kc_ref[pl.ds(pl.multiple_of(j * qb, qb), qb), :]
        return lax.dot_general(q_scr[...], kblk, (((1,), (1,)), ((), ())), preferred_element_type=F32)

    def lane_max(s):
        return jnp.maximum(s[:, :LANES], s[:, LANES:])

    mx_scr[...] = jnp.full((rows, LANES), -jnp.inf, F32)

    def pass1(j, carry):
        s = scores(j)
        s_scr[j] = s
        mx_scr[...] = jnp.maximum(mx_scr[...], lane_max(s))
        return carry

    lax.fori_loop(0, i, pass1, 0)
    r = lax.broadcasted_iota(jnp.int32, (qb, qb), 0)
    c = lax.broadcasted_iota(jnp.int32, (qb, qb), 1)
    visible = c // CHUNK <= r // CHUNK
    s = scores(i)
    s = jnp.concatenate([jnp.where(visible, s[h * qb:(h + 1) * qb], NEG_INF) for h in range(MLA_HEADS)],
                        axis=0)
    s_scr[i] = s
    m = jnp.max(jnp.maximum(mx_scr[...], lane_max(s)), axis=-1, keepdims=True)
    mx_scr[...] = jnp.broadcast_to(m, (rows, LANES))

    ls_scr[...] = jnp.zeros((rows, LANES), F32)
    acc_scr[...] = jnp.zeros((rows, MLA_KV_RANK), F32)

    def pass2(j, carry):
        sj = s_scr[j]
        mb = mx_scr[...]
        p = jnp.concatenate([jnp.exp2(sj[:, :LANES] - mb), jnp.exp2(sj[:, LANES:] - mb)], axis=1)
        ls_scr[...] = ls_scr[...] + p[:, :LANES] + p[:, LANES:]
        vblk = kc_ref[pl.ds(pl.multiple_of(j * qb, qb), qb), 0:MLA_KV_RANK]
        acc_scr[...] = acc_scr[...] + jnp.dot(p.astype(BF16), vblk, preferred_element_type=F32)
        return carry

    lax.fori_loop(0, i + 1, pass2, 0)

    o = acc_scr[...] * (1.0 / jnp.sum(ls_scr[...], axis=-1, keepdims=True))
    for h in range(MLA_HEADS):
        o_ref[:, h * MLA_KV_RANK:(h + 1) * MLA_KV_RANK] = o[h * qb:(h + 1) * qb, :].astype(BF16)


def _attention(cq, kc, w_abs, w_pe, w_per, cos4, sin4):
    b, s, _ = cq.shape
    qb = Q_BLK
    rows = MLA_HEADS * qb
    full = lambda a: pl.BlockSpec(a.shape, lambda bi, i: (0,) * a.ndim)
    pos = pl.BlockSpec((qb, LANES), lambda bi, i: (i, 0))
    return pl.pallas_call(
        _attn_kernel,
        grid=(b, s // qb),
        in_specs=[pl.BlockSpec((None, qb, MLA_Q_RANK), lambda bi, i: (bi, i, 0)),
                  pl.BlockSpec((None, s, 2 * LANES), lambda bi, i: (bi, 0, 0)),
                  full(w_abs), full(w_pe), full(w_per), pos, pos],
        out_specs=pl.BlockSpec((None, qb, MLA_HEADS * MLA_KV_RANK), lambda bi, i: (bi, i, 0)),
        out_shape=jax.ShapeDtypeStruct((b, s, MLA_HEADS * MLA_KV_RANK), BF16),
        scratch_shapes=[pltpu.VMEM((rows, 2 * LANES), BF16),
                        pltpu.VMEM((s // qb, rows, qb), F32),
                        pltpu.VMEM((rows, LANES), F32),
                        pltpu.VMEM((rows, LANES), F32),
                        pltpu.VMEM((rows, MLA_KV_RANK), F32)],
        compiler_params=_cparams(("arbitrary", "arbitrary")),
        name="attn",
    )(cq, kc, w_abs, w_pe, w_per, cos4, sin4)


def _ssd_kernel(xbc_ref, dt_ref, z_ref, alog_ref, dfull_ref, norm_ref, e_ref, o_ref, state_scr):
    c = pl.program_id(1)
    ln = xbc_ref.shape[0]
    gw = SSD_INNER // SSD_GROUPS
    hpg = SSD_HEADS // SSD_GROUPS

    @pl.when(c == 0)
    def _():
        state_scr[...] = jnp.zeros(state_scr.shape, F32)

    xs = xbc_ref[:, 0:SSD_INNER].astype(F32)
    dt = dt_ref[...]
    lane1 = lax.broadcasted_iota(jnp.int32, (1, LANES), 1)
    a = jnp.where(lane1 < SSD_HEADS, -jnp.exp(alog_ref[...]), 0.0)
    adt = dt * a
    row = lax.broadcasted_iota(jnp.int32, (ln, ln), 0)
    col = lax.broadcasted_iota(jnp.int32, (ln, ln), 1)
    causal = col <= row
    tril = causal.astype(BF16)
    e = e_ref[...]
    acs = sum(jnp.dot(tril, p, preferred_element_type=F32) for p in _split3(adt))
    acs_e = sum(jnp.dot(p, e, preferred_element_type=F32) for p in _split3(acs))
    dt_e = sum(jnp.dot(p, e, preferred_element_type=F32) for p in _split3(dt))
    acs_end = acs_e[ln - 1:ln, :]
    xdt = xs * dt_e
    x_end = (xdt * jnp.exp(acs_end - acs_e)).astype(BF16)
    eacs = jnp.exp(acs_e)
    chunk_decay = jnp.exp(acs_end)
    acs_t = acs.T
    lane_g = lax.broadcasted_iota(jnp.int32, (ln, gw), 1)

    ys = []
    for g in range(SSD_GROUPS):
        bg = xbc_ref[:, SSD_INNER + g * SSD_STATE:SSD_INNER + (g + 1) * SSD_STATE]
        cg = xbc_ref[:, SSD_INNER + SSD_GROUPS * SSD_STATE + g * SSD_STATE:
                     SSD_INNER + SSD_GROUPS * SSD_STATE + (g + 1) * SSD_STATE]
        cb = lax.dot_general(cg, bg, (((1,), (1,)), ((), ())), preferred_element_type=F32)
        prev = state_scr[g]
        y = jnp.dot(cg, prev.astype(BF16), preferred_element_type=F32) * eacs[:, g * gw:(g + 1) * gw]
        xg = xdt[:, g * gw:(g + 1) * gw]
        for hh in range(hpg):
            h = g * hpg + hh
            seg = acs[:, h:h + 1] - acs_t[h:h + 1, :]
            dec = jnp.exp(jnp.where(causal, seg, -jnp.inf))
            xm = jnp.where(lane_g // SSD_HEADDIM == hh, xg, 0.0).astype(BF16)
            y = y + jnp.dot((cb * dec).astype(BF16), xm, preferred_element_type=F32)
        st = lax.dot_general(bg, x_end[:, g * gw:(g + 1) * gw], (((0,), (0,)), ((), ())),
                             preferred_element_type=F32)
        state_scr[g] = chunk_decay[:, g * gw:(g + 1) * gw] * prev + st
        ys.append(y)

    y = jnp.concatenate(ys, axis=1) + dfull_ref[...] * xs
    zf = z_ref[...].astype(F32)
    yz = y * (zf * _sigmoid(zf))
    outs = []
    for g in range(SSD_GROUPS):
        yg = yz[:, g * gw:(g + 1) * gw]
        outs.append(yg * lax.rsqrt(jnp.mean(yg * yg, axis=-1, keepdims=True) + EPS))
    o_ref[...] = (jnp.concatenate(outs, axis=1) * norm_ref[...]).astype(BF16)


def _ssd(xbc, dt, z, alog_p, d_full, ssd_norm, expand):
    b, s, _ = xbc.shape
    ln = SSD_L
    tok = lambda w: pl.BlockSpec((None, ln, w), lambda bi, c: (bi, c, 0))
    full = lambda a: pl.BlockSpec(a.shape, lambda bi, c: (0,) * a.ndim)
    return pl.pallas_call(
        _ssd_kernel,
        grid=(b, s // ln),
        in_specs=[tok(SSD_CONV_DIM), tok(LANES), tok(SSD_INNER), full(alog_p), full(d_full), full(ssd_norm),
                  full(expand)],
        out_specs=tok(SSD_INNER),
        out_shape=jax.ShapeDtypeStruct((b, s, SSD_INNER), BF16),
        scratch_shapes=[pltpu.VMEM((SSD_GROUPS, SSD_STATE, SSD_INNER // SSD_GROUPS), F32)],
        compiler_params=_cparams(("arbitrary", "arbitrary")),
        name="ssd",
    )(xbc, dt, z, alog_p, d_full, ssd_norm, expand)


def _memkv_kernel(mem_ref, wk_ref, wv_ref, k_ref, v_ref):
    m = mem_ref[...].astype(BF16)
    k_ref[...] = jnp.dot(m, wk_ref[...], preferred_element_type=F32).astype(BF16)
    v_ref[...] = jnp.dot(m, wv_ref[...], preferred_element_type=F32).astype(BF16)


def _memkv(mem, wk, wv):
    b, m, d = mem.shape
    blk = pl.BlockSpec((None, m, d), lambda bi: (bi, 0, 0))
    full = lambda a: pl.BlockSpec(a.shape, lambda bi: (0,) * a.ndim)
    return pl.pallas_call(
        _memkv_kernel,
        grid=(b,),
        in_specs=[blk, full(wk), full(wv)],
        out_specs=(blk, blk),
        out_shape=(jax.ShapeDtypeStruct((b, m, d), BF16), jax.ShapeDtypeStruct((b, m, d), BF16)),
        compiler_params=_cparams(("arbitrary",)),
        name="memkv",
    )(mem, wk, wv)


def _split_bf16(w):
    hi = w.astype(BF16)
    return hi, (w - hi.astype(F32)).astype(BF16)


def _router_logits(h, wrh_ref, wrl_ref, br_ref):
    h_hi, h_lo = _split_bf16(h)
    return (jnp.dot(h_hi, wrh_ref[...], preferred_element_type=F32)
            + jnp.dot(h_hi, wrl_ref[...], preferred_element_type=F32)
            + jnp.dot(h_lo, wrh_ref[...], preferred_element_type=F32)) + br_ref[...]


def _is_group_lane(lane):
    return (lane >= N_EXPERTS) & (lane < N_EXPERTS + N_EXPERT_GROUPS)


def _mid_kernel(x_ref, ol_ref, ss_ref, wmix_ref, g1_ref, b1_ref, wq_ref, km_ref, vm_ref, wo_ref,
                g2_ref, b2_ref, wrh_ref, wrl_ref, br_ref, row_ref, meta_ref, cnt_ref, run_scr):
    step = pl.program_id(0) * pl.num_programs(1) + pl.program_id(1)
    tm = x_ref.shape[0]
    n_lat = MLA_HEADS * MLA_KV_RANK

    @pl.when(step == 0)
    def _():
        run_scr[...] = jnp.zeros(run_scr.shape, F32)

    mix = (jnp.dot(ol_ref[...], wmix_ref[0:n_lat, :], preferred_element_type=F32)
           + jnp.dot(ss_ref[...], wmix_ref[n_lat:, :], preferred_element_type=F32))
    h1 = _layer_norm(ALPHA * x_ref[...] + mix, g1_ref[...], b1_ref[...])

    q = jnp.dot(h1.astype(BF16), wq_ref[...], preferred_element_type=F32).astype(BF16)
    outs = []
    for h in range(XA_HEADS):
        sl = slice(h * XA_HEAD_DIM, (h + 1) * XA_HEAD_DIM)
        s = lax.dot_general(q[:, sl], km_ref[:, sl], (((1,), (1,)), ((), ())), preferred_element_type=F32)
        p = jnp.exp(s - jnp.max(s, axis=-1, keepdims=True))
        o = jnp.dot(p.astype(BF16), vm_ref[:, sl], preferred_element_type=F32)
        outs.append((o / jnp.sum(p, axis=-1, keepdims=True)).astype(BF16))
    xa = jnp.dot(jnp.concatenate(outs, axis=1), wo_ref[...], preferred_element_type=F32)
    h2 = _layer_norm(ALPHA * h1 + xa, g2_ref[...], b2_ref[...])

    logits = _router_logits(h2, wrh_ref, wrl_ref, br_ref)
    lane = lax.broadcasted_iota(jnp.int32, (tm, LANES), 1)
    is_g = _is_group_lane(lane)
    gmax = jnp.max(jnp.where(is_g, logits, -jnp.inf), axis=-1, keepdims=True)
    g_idx = jnp.min(jnp.where(is_g & (logits == gmax), lane - N_EXPERTS, LANES), axis=-1, keepdims=True)
    onehot = (lane == g_idx).astype(BF16)
    r = lax.broadcasted_iota(jnp.int32, (tm, tm), 0)
    c = lax.broadcasted_iota(jnp.int32, (tm, tm), 1)
    before = jnp.dot((c < r).astype(BF16), onehot, preferred_element_type=F32) + run_scr[...]
    rank = jnp.sum(jnp.where(lane == g_idx, before, 0.0), axis=-1, keepdims=True)
    run_scr[...] = run_scr[...] + jnp.sum(onehot.astype(F32), axis=0, keepdims=True)
    cnt_ref[...] = run_scr[...]
    meta_ref[...] = jnp.where(lane == 0, g_idx.astype(F32), jnp.where(lane == 1, rank, 0.0))

    for k in range(ROW_SUB):
        row_ref[_lane_block(k, tm), :] = h2[:, k * LANES:(k + 1) * LANES]


def _mid(x, o_lat, ssd_out, w_mix, g1, b1, wq, kmem, vmem, wo, g2, b2, wrh, wrl, br):
    b, s, d = x.shape
    tm = TM_MID
    tok = lambda w: pl.BlockSpec((None, tm, w), lambda bi, j: (bi, j, 0))
    full = lambda a: pl.BlockSpec(a.shape, lambda bi, j: (0,) * a.ndim)
    mem = pl.BlockSpec((None,) + kmem.shape[1:], lambda bi, j: (bi, 0, 0))
    tok_rows = lambda n: pl.BlockSpec((None, n * ROW_SUB, LANES), lambda bi, j: (bi, j, 0))
    return pl.pallas_call(
        _mid_kernel,
        grid=(b, s // tm),
        in_specs=[tok(d), tok(o_lat.shape[-1]), tok(SSD_INNER), full(w_mix), full(g1), full(b1), full(wq),
                  mem, mem, full(wo), full(g2), full(b2), full(wrh), full(wrl), full(br)],
        out_specs=(tok_rows(tm), tok(LANES), pl.BlockSpec((1, LANES), lambda bi, j: (0, 0))),
        out_shape=(jax.ShapeDtypeStruct((b, s * ROW_SUB, LANES), F32),
                   jax.ShapeDtypeStruct((b, s, LANES), F32),
                   jax.ShapeDtypeStruct((1, LANES), F32)),
        scratch_shapes=[pltpu.VMEM((1, LANES), F32)],
        compiler_params=_cparams(("arbitrary", "arbitrary")),
        name="mid",
    )(x, o_lat, ssd_out, w_mix, g1, b1, wq, kmem, vmem, wo, g2, b2, wrh, wrl, br)


def _scatter_kernel(dest_ref, padrow_ref, rows_ref, out_hbm, zero_scr, sem, psem):
    c = pl.program_id(0)
    n = dest_ref.shape[1]

    def issue(t, carry):
        pltpu.make_async_copy(rows_ref.at[_row(t)], out_hbm.at[_row(dest_ref[0, t])], sem).start()
        return carry

    lax.fori_loop(0, n, issue, 0)

    npad = padrow_ref.shape[1]

    @pl.when(c == 0)
    def _():
        zero_scr[...] = jnp.zeros(zero_scr.shape, F32)

        def issue_pad(k, carry):
            pltpu.make_async_copy(zero_scr, out_hbm.at[_row(padrow_ref[0, k])], psem).start()
            return carry

        lax.fori_loop(0, npad, issue_pad, 0)

        def wait_pad(k, carry):
            pltpu.make_async_copy(zero_scr, out_hbm.at[_row(0)], psem).wait()
            return carry

        lax.fori_loop(0, npad, wait_pad, 0)

    pltpu.make_async_copy(rows_ref, out_hbm.at[pl.ds(0, n * ROW_SUB)], sem).wait()


def _scatter_rows(rows, dest3, padrow, n_out):
    t = rows.shape[0] // ROW_SUB
    n = dest3.shape[-1]
    return pl.pallas_call(
        _scatter_kernel,
        grid=(t // n,),
        in_specs=[pl.BlockSpec((None, 1, n), lambda c: (c, 0, 0), memory_space=pltpu.SMEM),
                  pl.BlockSpec(padrow.shape, lambda c: (0, 0), memory_space=pltpu.SMEM),
                  pl.BlockSpec((n * ROW_SUB, LANES), lambda c: (c, 0))],
        out_specs=pl.BlockSpec(memory_space=pl.ANY),
        out_shape=jax.ShapeDtypeStruct((n_out * ROW_SUB, LANES), rows.dtype),
        scratch_shapes=[pltpu.VMEM((ROW_SUB, LANES), F32), pltpu.SemaphoreType.DMA(()),
                        pltpu.SemaphoreType.DMA(())],
        compiler_params=_cparams(("arbitrary",)),
        name="scatter",
    )(dest3, padrow, rows)


def _unsort_kernel(dcur_ref, dnext_ref, rows_hbm, o_ref, buf, sem):
    c = pl.program_id(0)
    n = o_ref.shape[0]
    slot = c % 2

    def issue(dref, sl):
        def body(t, carry):
            pltpu.make_async_copy(rows_hbm.at[_row(dref[0, t])], buf.at[sl, _row(t)], sem.at[sl]).start()
            return carry

        lax.fori_loop(0, n, body, 0)

    @pl.when(c == 0)
    def _():
        issue(dcur_ref, 0)

    @pl.when(c + 1 < pl.num_programs(0))
    def _():
        issue(dnext_ref, 1 - slot)

    pltpu.make_async_copy(rows_hbm.at[pl.ds(0, n * ROW_SUB)], buf.at[slot], sem.at[slot]).wait()
    for k in range(ROW_SUB):
        o_ref[:, k * LANES:(k + 1) * LANES] = buf[slot, _lane_block(k, n), :]


def _unsort_rows(rows, dest3, n_tok):
    n = dest3.shape[-1]
    nc = n_tok // n
    return pl.pallas_call(
        _unsort_kernel,
        grid=(nc,),
        in_specs=[pl.BlockSpec((None, 1, n), lambda c: (c, 0, 0), memory_space=pltpu.SMEM),
                  pl.BlockSpec((None, 1, n), lambda c: (jnp.minimum(c + 1, nc - 1), 0, 0),
                               memory_space=pltpu.SMEM),
                  pl.BlockSpec(memory_space=pl.ANY)],
        out_specs=pl.BlockSpec((n, D_MODEL), lambda c: (c, 0)),
        out_shape=jax.ShapeDtypeStruct((n_tok, D_MODEL), rows.dtype),
        scratch_shapes=[pltpu.VMEM((2, n * ROW_SUB, LANES), F32), pltpu.SemaphoreType.DMA((2,))],
        compiler_params=_cparams(("arbitrary",)),
        name="unsort",
    )(dest3, dest3, rows)


def _moe_kernel(gid_ref, nused_ref, rows_ref, wrh_ref, wrl_ref, br_ref, wg_ref, wu_ref, wd_ref, g3_ref, b3_ref,
                o_ref):
    i = pl.program_id(0)
    tm = rows_ref.shape[0] // ROW_SUB

    @pl.when(i < nused_ref[0])
    def _():
        g = gid_ref[i]
        x = jnp.concatenate([rows_ref[_lane_block(k, tm), :] for k in range(ROW_SUB)], axis=1)

        logits = _router_logits(x, wrh_ref, wrl_ref, br_ref)
        lane = lax.broadcasted_iota(jnp.int32, (tm, LANES), 1)
        is_g = _is_group_lane(lane)
        gmax = jnp.max(jnp.where(is_g, logits, -jnp.inf), axis=-1, keepdims=True)
        gsel = jnp.sum(jnp.where(lane == N_EXPERTS + g, logits, 0.0), axis=-1, keepdims=True)
        g_gate = jnp.exp(gsel - gmax) / jnp.sum(jnp.where(is_g, jnp.exp(logits - gmax), 0.0),
                                                axis=-1, keepdims=True)
        in_grp = (lane < N_EXPERTS) & (lane // EXPERTS_PER_GROUP == g)
        t1 = jnp.max(jnp.where(in_grp, logits, -jnp.inf), axis=-1, keepdims=True)
        i1 = jnp.min(jnp.where(in_grp & (logits == t1), lane, LANES), axis=-1, keepdims=True)
        rest = in_grp & (lane != i1)
        t2 = jnp.max(jnp.where(rest, logits, -jnp.inf), axis=-1, keepdims=True)
        i2 = jnp.min(jnp.where(rest & (logits == t2), lane, LANES), axis=-1, keepdims=True)
        e21 = jnp.exp(t2 - t1)
        w1 = g_gate / (1.0 + e21)
        comb = jnp.where(lane == i1, w1, 0.0) + jnp.where(lane == i2, w1 * e21, 0.0)

        xb = x.astype(BF16)
        acc = jnp.zeros(x.shape, F32)
        for e in range(EXPERTS_PER_GROUP):
            ce = jnp.sum(jnp.where(lane == g * EXPERTS_PER_GROUP + e, comb, 0.0), axis=-1, keepdims=True)
            hg = jnp.dot(xb, wg_ref[e], preferred_element_type=F32)
            hu = jnp.dot(xb, wu_ref[e], preferred_element_type=F32)
            hd = hg * _sigmoid(hg) * hu * ce
            acc = acc + jnp.dot(hd.astype(BF16), wd_ref[e], preferred_element_type=F32)
        y = _layer_norm(ALPHA * x + acc, g3_ref[...], b3_ref[...])
        for k in range(ROW_SUB):
            o_ref[_lane_block(k, tm), :] = y[:, k * LANES:(k + 1) * LANES]

    @pl.when(i >= nused_ref[0])
    def _():
        o_ref[...] = jnp.zeros(o_ref.shape, F32)


def _moe(rows_sorted, tile_gid, n_used, wrh, wrl, br, wg, wu, wd, g3, b3):
    n_rows = rows_sorted.shape[0] // ROW_SUB
    tm = TM_MOE
    nt = n_rows // tm

    def tile_map(i, gid, nused):
        return (jnp.maximum(jnp.minimum(i, nused[0] - 1), 0), 0)

    def w_map(i, gid, nused):
        return (gid[i], 0, 0)

    full = lambda a: pl.BlockSpec(a.shape, lambda i, gid, nused: (0,) * a.ndim)
    e = EXPERTS_PER_GROUP
    grid_spec = pltpu.PrefetchScalarGridSpec(
        num_scalar_prefetch=2,
        grid=(nt,),
        in_specs=[pl.BlockSpec((tm * ROW_SUB, LANES), tile_map), full(wrh), full(wrl), full(br),
                  pl.BlockSpec((e, D_MODEL, D_EXPERT), w_map),
                  pl.BlockSpec((e, D_MODEL, D_EXPERT), w_map),
                  pl.BlockSpec((e, D_EXPERT, D_MODEL), w_map),
                  full(g3), full(b3)],
        out_specs=pl.BlockSpec((tm * ROW_SUB, LANES), lambda i, gid, nused: (i, 0)),
    )
    return pl.pallas_call(
        _moe_kernel,
        grid_spec=grid_spec,
        out_shape=jax.ShapeDtypeStruct((n_rows * ROW_SUB, LANES), F32),
        compiler_params=_cparams(("arbitrary",)),
        name="moe",
    )(tile_gid, n_used, rows_sorted, wrh, wrl, br, wg, wu, wd, g3, b3)


def _rope_tables(seq):
    pos = jnp.arange(seq, dtype=F32)
    inv_freq = ROPE_THETA ** (-jnp.arange(0, MLA_ROPE, 2, dtype=F32) / MLA_ROPE)
    ang = pos[:, None] * inv_freq[None, :]
    cos = jnp.tile(jnp.cos(ang), (1, LANES // (MLA_ROPE // 2)))
    sin = jnp.tile(jnp.sin(ang), (1, LANES // (MLA_ROPE // 2)))
    return cos, sin


def _pad_lanes(v, n=LANES):
    v = v.reshape(1, -1)
    return jnp.pad(v, ((0, 0), (0, n - v.shape[1])))


def kernel(x, mem, w_in, mla_q_norm, w_q_up, mla_kv_norm, w_kv_up, ssd_conv_w, ssd_conv_b, ssd_dt_bias, ssd_a_log, ssd_d, ssd_norm, w_out, ln1_g, ln1_b, xa_wq, xa_wk, xa_wv, xa_wo, ln2_g, ln2_b, router_group_w, router_group_b, router_expert_w, router_expert_b, expert_w_gate, expert_w_up, expert_w_down, ln3_g, ln3_b):
    b, s, d = x.shape
    t = b * s
    l = 0
    row = lambda v: v[l].reshape(1, -1)

    wi = w_in[l]
    o1 = MLA_Q_RANK
    o2 = o1 + MLA_KV_RANK
    o3 = o2 + MLA_ROPE
    o4 = o3 + SSD_INNER
    o5 = o4 + SSD_CONV_DIM
    w_kr = wi[:, o2:o3]
    half = MLA_ROPE // 2
    w_kr_rot = jnp.concatenate([-w_kr[:, half:], w_kr[:, :half]], axis=1)
    reps = LANES // MLA_ROPE
    w_in_r = jnp.concatenate(
        [wi[:, :o2], wi[:, o3:o4], wi[:, o4:o5], jnp.tile(w_kr, (1, reps)), jnp.tile(w_kr_rot, (1, reps)),
         wi[:, o5:], jnp.zeros((d, LANES - SSD_HEADS), F32)], axis=1).astype(BF16)
    cos4, sin4 = _rope_tables(s)
    attn_scale = (MLA_NOPE + MLA_ROPE) ** -0.5 * math.log2(math.e)
    w_abs, w_pe, w_per, w_mix = _fold_weights(w_q_up[l], w_kv_up[l], mla_kv_norm[l], w_out[l], attn_scale)
    expand = (jnp.arange(LANES)[:, None] == (jnp.arange(SSD_INNER)[None, :] // SSD_HEADDIM)).astype(BF16)
    d_full = jnp.repeat(ssd_d[l], SSD_HEADDIM).reshape(1, SSD_INNER)
    wrh, wrl = _split_bf16(jnp.concatenate(
        [router_expert_w[l], router_group_w[l], jnp.zeros((d, LANES - N_EXPERTS - N_EXPERT_GROUPS), F32)], axis=1))
    br = _pad_lanes(jnp.concatenate([router_expert_b[l], router_group_b[l]]))

    cq, kc, z, xbc, dt = _inproj(x, w_in_r, row(mla_q_norm), ssd_conv_w[l], row(ssd_conv_b),
                                 _pad_lanes(ssd_dt_bias[l]), cos4, sin4)
    o_lat = _attention(cq, kc, w_abs, w_pe, w_per, cos4, sin4)
    ssd_out = _ssd(xbc, dt, z, _pad_lanes(ssd_a_log[l]), d_full, row(ssd_norm), expand)

    kmem, vmem = _memkv(mem, xa_wk[l].astype(BF16), xa_wv[l].astype(BF16))
    wq = (xa_wq[l] * (XA_HEAD_DIM ** -0.5)).astype(BF16)
    rows, meta, counts = _mid(x, o_lat, ssd_out, w_mix, row(ln1_g), row(ln1_b), wq, kmem, vmem,
                              xa_wo[l].astype(BF16), row(ln2_g), row(ln2_b), wrh, wrl, br)

    tm = TM_MOE
    nt = t // tm + N_EXPERT_GROUPS - 1
    meta = meta.reshape(t, LANES)
    g_idx = meta[:, 0].astype(jnp.int32)
    rank = meta[:, 1].astype(jnp.int32)
    cnt = counts[0, :N_EXPERT_GROUPS].astype(jnp.int32)
    ntile = (cnt + tm - 1) // tm
    tile_end = jnp.cumsum(ntile)
    tile_start = tile_end - ntile
    dest = tile_start[g_idx] * tm + rank
    n_used = tile_end[-1:].astype(jnp.int32)
    tile_gid = jnp.minimum(jnp.sum(jnp.arange(nt)[:, None] >= tile_end[None, :], axis=1),
                           N_EXPERT_GROUPS - 1).astype(jnp.int32)
    pad_end = jnp.cumsum(ntile * tm - cnt)
    k = jnp.arange(nt * tm - t)
    kg = jnp.sum(k[:, None] >= pad_end[None, :], axis=1)
    kgc = jnp.minimum(kg, N_EXPERT_GROUPS - 1)
    off = k - jnp.where(kg > 0, pad_end[jnp.maximum(kg, 1) - 1], 0)
    padrow = jnp.where(kg < N_EXPERT_GROUPS, tile_start[kgc] * tm + cnt[kgc] + off,
                       tile_end[-1] * tm + off).astype(jnp.int32).reshape(1, -1)
    sc_chunk = min(SCATTER_CHUNK, t)
    un_chunk = min(UNSORT_CHUNK, t)

    rows_sorted = _scatter_rows(rows.reshape(t * ROW_SUB, LANES), dest.reshape(t // sc_chunk, 1, sc_chunk),
                                padrow, nt * tm)
    out_sorted = _moe(rows_sorted, tile_gid, n_used, wrh, wrl, br, expert_w_gate[l].astype(BF16),
                      expert_w_up[l].astype(BF16), expert_w_down[l].astype(BF16), row(ln3_g), row(ln3_b))
    out = _unsort_rows(out_sorted, dest.reshape(t // un_chunk, 1, un_chunk), t)
    return out.reshape(b, s, d)
```

```python
import functools
import math

import jax
import jax.numpy as jnp
from jax import lax
from jax.experimental import pallas as pl
from jax.experimental.pallas import tpu as pltpu

F32 = jnp.float32
BF16 = jnp.bfloat16
HIGHEST = lax.Precision.HIGHEST

EPS = 1e-5
NEG_INF = -1e30
CHUNK = 64
ROPE_THETA = 10000.0

D_MODEL = 1024
MLA_HEADS = 8
MLA_NOPE = 64
MLA_ROPE = 32
MLA_V = 64
MLA_Q_RANK = 256
MLA_KV_RANK = 128
SSD_HEADS = 8
SSD_HEADDIM = 64
SSD_INNER = 512
SSD_GROUPS = 2
SSD_STATE = 128
SSD_CONV = 4
SSD_CONV_DIM = 1024
XA_HEADS = 4
XA_HEAD_DIM = 256
N_EXPERT_GROUPS = 4
EXPERTS_PER_GROUP = 8
N_EXPERTS = 32
D_EXPERT = 256
DEPTH = 1
ALPHA = (2.0 * DEPTH) ** 0.25

LANES = 128
V7X_VMEM_LIMIT = 56 * 1024 * 1024

C_CQ = 0
C_CKV = 256
C_Z = 384
C_XBC = 896
C_KPE = 1920
C_KPER = 2048
C_DT = 2176
IN_COLS_R = 2304

TM_IN = 512
Q_BLK = 256
SSD_L = 256
TM_MID = 512
MID_SUB = 512
TM_MOE = 512
SCATTER_CHUNK = 2048
UNSORT_CHUNK = 1024
ROW_SUB = D_MODEL // LANES


def _cparams(sem):
    return pltpu.CompilerParams(dimension_semantics=sem, vmem_limit_bytes=V7X_VMEM_LIMIT)


def _sigmoid(x):
    return 1.0 / (1.0 + jnp.exp(-x))


def _split3(x):
    hi = x.astype(BF16)
    r = x - hi.astype(F32)
    mid = r.astype(BF16)
    return hi, mid, (r - mid.astype(F32)).astype(BF16)


def _row(t):
    return pl.ds(pl.multiple_of(t * ROW_SUB, ROW_SUB), ROW_SUB)


def _lane_block(k, n):
    return pl.ds(k, n, stride=ROW_SUB)


def _layer_norm(x, g, b):
    mu = jnp.mean(x, axis=-1, keepdims=True)
    xc = x - mu
    var = jnp.mean(xc * xc, axis=-1, keepdims=True)
    return xc * lax.rsqrt(var + EPS) * g + b


def _fold_kernel(wqn_ref, wukt_ref, wuv_ref, woa_ref, gkv_row_ref, gkv_col_ref, wabs_ref, wof_ref, *, scale):
    for h in range(MLA_HEADS):
        wabs_ref[h] = scale * jnp.dot(wqn_ref[h], wukt_ref[h] * gkv_row_ref[...],
                                      precision=HIGHEST, preferred_element_type=F32)
        wof_ref[h] = jnp.dot(wuv_ref[h] * gkv_col_ref[...], woa_ref[h],
                             precision=HIGHEST, preferred_element_type=F32)


def _fold_weights(w_q_up, w_kv_up, mla_kv_norm, w_out, scale):
    wq = w_q_up.reshape(MLA_Q_RANK, MLA_HEADS, MLA_NOPE + MLA_ROPE)
    wkv = w_kv_up.reshape(MLA_KV_RANK, MLA_HEADS, MLA_NOPE + MLA_V)
    wqn = jnp.transpose(wq[:, :, :MLA_NOPE], (1, 0, 2))
    wukt = jnp.transpose(wkv[:, :, :MLA_NOPE], (1, 2, 0))
    wuv = jnp.transpose(wkv[:, :, MLA_NOPE:], (1, 0, 2))
    woa = w_out[:MLA_HEADS * MLA_V].reshape(MLA_HEADS, MLA_V, D_MODEL)
    wabs, wof = pl.pallas_call(
        functools.partial(_fold_kernel, scale=scale),
        out_shape=(jax.ShapeDtypeStruct((MLA_HEADS, MLA_Q_RANK, MLA_KV_RANK), F32),
                   jax.ShapeDtypeStruct((MLA_HEADS, MLA_KV_RANK, D_MODEL), F32)),
        name="fold",
    )(wqn, wukt, wuv, woa, mla_kv_norm.reshape(1, MLA_KV_RANK), mla_kv_norm.reshape(MLA_KV_RANK, 1))
    w_abs = jnp.transpose(wabs, (1, 0, 2)).reshape(MLA_Q_RANK, MLA_HEADS * MLA_KV_RANK).astype(BF16)
    w_pe3 = wq[:, :, MLA_NOPE:] * scale
    half = MLA_ROPE // 2
    w_per3 = jnp.concatenate([-w_pe3[:, :, half:], w_pe3[:, :, :half]], axis=-1)
    w_pe = w_pe3.reshape(MLA_Q_RANK, MLA_HEADS * MLA_ROPE).astype(BF16)
    w_per = w_per3.reshape(MLA_Q_RANK, MLA_HEADS * MLA_ROPE).astype(BF16)
    w_mix = jnp.concatenate([wof.reshape(MLA_HEADS * MLA_KV_RANK, D_MODEL), w_out[MLA_HEADS * MLA_V:]],
                            axis=0).astype(BF16)
    return w_abs, w_pe, w_per, w_mix


def _inproj_kernel(x_ref, w_ref, gq_ref, cw_ref, cb_ref, dtb_ref, cos_ref, sin_ref,
                   cq_ref, kc_ref, z_ref, xbc_ref, dt_ref, cbuf):
    j = pl.program_id(1)
    tm = x_ref.shape[0]
    @pl.when(j == 0)
    def _():
        cbuf[0:8, :] = jnp.zeros((8, SSD_CONV_DIM), F32)

    xb = x_ref[...].astype(BF16)
    seg = lambda lo, n: jnp.dot(xb, w_ref[:, lo:lo + n], preferred_element_type=F32)

    u = seg(C_XBC, SSD_CONV_DIM)
    cbuf[8:8 + tm, :] = u
    acc = cb_ref[...] + cw_ref[SSD_CONV - 1:SSD_CONV, :] * u
    for k in range(SSD_CONV - 1):
        acc = acc + cw_ref[k:k + 1, :] * cbuf[pl.ds(8 - (SSD_CONV - 1) + k, tm), :]
    xbc_ref[...] = (acc * _sigmoid(acc)).astype(BF16)
    cbuf[0:8, :] = cbuf[tm:tm + 8, :]

    z_ref[...] = seg(C_Z, SSD_INNER).astype(BF16)

    tail = seg(C_KPE, 3 * LANES)
    dtr = tail[:, 2 * LANES:] + dtb_ref[...]
    dt_ref[...] = jnp.maximum(dtr, 0.0) + jnp.log(1.0 + jnp.exp(-jnp.abs(dtr)))
    kpe = tail[:, :LANES] * cos_ref[...] + tail[:, LANES:2 * LANES] * sin_ref[...]

    lat = seg(C_CQ, MLA_Q_RANK + MLA_KV_RANK)
    c_q = lat[:, :MLA_Q_RANK]
    cq = c_q * lax.rsqrt(jnp.mean(c_q * c_q, axis=-1, keepdims=True) + EPS) * gq_ref[...]
    cq_ref[...] = cq.astype(BF16)
    c_kv = lat[:, MLA_Q_RANK:]
    ckv = c_kv * lax.rsqrt(jnp.mean(c_kv * c_kv, axis=-1, keepdims=True) + EPS)
    kc_ref[...] = jnp.concatenate([ckv, kpe], axis=1).astype(BF16)


def _inproj(x, w_in_r, gq, conv_w, conv_b, dt_bias_p, cos4, sin4):
    b, s, d = x.shape
    tm = TM_IN
    grid = (b, s // tm)
    tok = lambda w: pl.BlockSpec((None, tm, w), lambda bi, j: (bi, j, 0))
    full = lambda a: pl.BlockSpec(a.shape, lambda bi, j: (0,) * a.ndim)
    pos = pl.BlockSpec((tm, LANES), lambda bi, j: (j, 0))
    return pl.pallas_call(
        _inproj_kernel,
        grid=grid,
        in_specs=[tok(d), full(w_in_r), full(gq), full(conv_w), full(conv_b), full(dt_bias_p), pos, pos],
        out_specs=(tok(MLA_Q_RANK), tok(2 * LANES), tok(SSD_INNER), tok(SSD_CONV_DIM), tok(LANES)),
        out_shape=(jax.ShapeDtypeStruct((b, s, MLA_Q_RANK), BF16),
                   jax.ShapeDtypeStruct((b, s, 2 * LANES), BF16),
                   jax.ShapeDtypeStruct((b, s, SSD_INNER), BF16),
                   jax.ShapeDtypeStruct((b, s, SSD_CONV_DIM), BF16),
                   jax.ShapeDtypeStruct((b, s, LANES), F32)),
        scratch_shapes=[pltpu.VMEM((tm + 8, SSD_CONV_DIM), F32)],
        compiler_params=_cparams(("arbitrary", "arbitrary")),
        name="inproj",
    )(x, w_in_r, gq, conv_w, conv_b, dt_bias_p, cos4, sin4)


def _attn_kernel(cq_ref, kc_ref, wabs_ref, wpe_ref, wper_ref, cos_ref, sin_ref, o_ref,
                 q_scr, s_scr, mx_scr, ls_scr, acc_scr):
    i = pl.program_id(1)
    qb = cq_ref.shape[0]
    rows = MLA_HEADS * qb
    cq = cq_ref[...]
    qabs = jnp.dot(cq, wabs_ref[...], preferred_element_type=F32)
    cos8 = jnp.concatenate([cos_ref[...], cos_ref[...]], axis=1)
    sin8 = jnp.concatenate([sin_ref[...], sin_ref[...]], axis=1)
    qpe = (jnp.dot(cq, wpe_ref[...], preferred_element_type=F32) * cos8
           + jnp.dot(cq, wper_ref[...], preferred_element_type=F32) * sin8)
    lane = lax.broadcasted_iota(jnp.int32, (qb, LANES), 1)
    heads_per_blk = LANES // MLA_ROPE
    for h in range(MLA_HEADS):
        pe_blk = qpe[:, LANES * (h // heads_per_blk):LANES * (h // heads_per_blk + 1)]
        pe_h = jnp.where(lane // MLA_ROPE == h % heads_per_blk, pe_blk, 0.0)
        q_scr[h * qb:(h + 1) * qb, :] = jnp.concatenate(
            [qabs[:, h * MLA_KV_RANK:(h + 1) * MLA_KV_RANK], pe_h], axis=1).astype(BF16)

    def scores(j):
        kblk = kc_ref[pl.ds(pl.multiple_of(j * qb, qb), qb), :]
        return lax.dot_general(q_scr[...], kblk, (((1,), (1,)), ((), ())), preferred_element_type=F32)

    def lane_max(s):
        return jnp.maximum(s[:, :LANES], s[:, LANES:])

    mx_scr[...] = jnp.full((rows, LANES), -jnp.inf, F32)

    def pass1(j, carry):
        s = scores(j)
        s_scr[j] = s
        mx_scr[...] = jnp.maximum(mx_scr[...], lane_max(s))
        return carry

    lax.fori_loop(0, i, pass1, 0)
    r = lax.broadcasted_iota(jnp.int32, (qb, qb), 0)
    c = lax.broadcasted_iota(jnp.int32, (qb, qb), 1)
    visible = c // CHUNK <= r // CHUNK
    s = scores(i)
    s = jnp.concatenate([jnp.where(visible, s[h * qb:(h + 1) * qb], NEG_INF) for h in range(MLA_HEADS)],
                        axis=0)
    s_scr[i] = s
    m = jnp.max(jnp.maximum(mx_scr[...], lane_max(s)), axis=-1, keepdims=True)
    mx_scr[...] = jnp.broadcast_to(m, (rows, LANES))

    ls_scr[...] = jnp.zeros((rows, LANES), F32)
    acc_scr[...] = jnp.zeros((rows, MLA_KV_RANK), F32)

    def pass2(j, carry):
        sj = s_scr[j]
        mb = mx_scr[...]
        p = jnp.concatenate([jnp.exp2(sj[:, :LANES] - mb), jnp.exp2(sj[:, LANES:] - mb)], axis=1)
        ls_scr[...] = ls_scr[...] + p[:, :LANES] + p[:, LANES:]
        vblk = kc_ref[pl.ds(pl.multiple_of(j * qb, qb), qb), 0:MLA_KV_RANK]
        acc_scr[...] = acc_scr[...] + jnp.dot(p.astype(BF16), vblk, preferred_element_type=F32)
        return carry

    lax.fori_loop(0, i + 1, pass2, 0)

    o = acc_scr[...] * (1.0 / jnp.sum(ls_scr[...], axis=-1, keepdims=True))
    for h in range(MLA_HEADS):
        o_ref[:, h * MLA_KV_RANK:(h + 1) * MLA_KV_RANK] = o[h * qb:(h + 1) * qb, :].astype(BF16)


def _attention(cq, kc, w_abs, w_pe, w_per, cos4, sin4):
    b, s, _ = cq.shape
    qb = Q_BLK
    rows = MLA_HEADS * qb
    full = lambda a: pl.BlockSpec(a.shape, lambda bi, i: (0,) * a.ndim)
    pos = pl.BlockSpec((qb, LANES), lambda bi, i: (i, 0))
    return pl.pallas_call(
        _attn_kernel,
        grid=(b, s // qb),
        in_specs=[pl.BlockSpec((None, qb, MLA_Q_RANK), lambda bi, i: (bi, i, 0)),
                  pl.BlockSpec((None, s, 2 * LANES), lambda bi, i: (bi, 0, 0)),
                  full(w_abs), full(w_pe), full(w_per), pos, pos],
        out_specs=pl.BlockSpec((None, qb, MLA_HEADS * MLA_KV_RANK), lambda bi, i: (bi, i, 0)),
        out_shape=jax.ShapeDtypeStruct((b, s, MLA_HEADS * MLA_KV_RANK), BF16),
        scratch_shapes=[pltpu.VMEM((rows, 2 * LANES), BF16),
                        pltpu.VMEM((s // qb, rows, qb), F32),
                        pltpu.VMEM((rows, LANES), F32),
                        pltpu.VMEM((rows, LANES), F32),
                        pltpu.VMEM((rows, MLA_KV_RANK), F32)],
        compiler_params=_cparams(("arbitrary", "arbitrary")),
        name="attn",
    )(cq, kc, w_abs, w_pe, w_per, cos4, sin4)


def _ssd_kernel(xbc_ref, dt_ref, z_ref, alog_ref, dfull_ref, norm_ref, e_ref, o_ref, state_scr):
    c = pl.program_id(1)
    ln = xbc_ref.shape[0]
    gw = SSD_INNER // SSD_GROUPS
    hpg = SSD_HEADS // SSD_GROUPS

    @pl.when(c == 0)
    def _():
        state_scr[...] = jnp.zeros(state_scr.shape, F32)

    xs = xbc_ref[:, 0:SSD_INNER].astype(F32)
    dt = dt_ref[...]
    lane1 = lax.broadcasted_iota(jnp.int32, (1, LANES), 1)
    a = jnp.where(lane1 < SSD_HEADS, -jnp.exp(alog_ref[...]), 0.0)
    adt = dt * a
    row = lax.broadcasted_iota(jnp.int32, (ln, ln), 0)
    col = lax.broadcasted_iota(jnp.int32, (ln, ln), 1)
    causal = col <= row
    tril = causal.astype(BF16)
    e = e_ref[...]
    acs = sum(jnp.dot(tril, p, preferred_element_type=F32) for p in _split3(adt))
    acs_e = sum(jnp.dot(p, e, preferred_element_type=F32) for p in _split3(acs))
    dt_e = sum(jnp.dot(p, e, preferred_element_type=F32) for p in _split3(dt))
    acs_end = acs_e[ln - 1:ln, :]
    xdt = xs * dt_e
    x_end = (xdt * jnp.exp(acs_end - acs_e)).astype(BF16)
    eacs = jnp.exp(acs_e)
    chunk_decay = jnp.exp(acs_end)
    acs_t = acs.T
    lane_g = lax.broadcasted_iota(jnp.int32, (ln, gw), 1)

    ys = []
    for g in range(SSD_GROUPS):
        bg = xbc_ref[:, SSD_INNER + g * SSD_STATE:SSD_INNER + (g + 1) * SSD_STATE]
        cg = xbc_ref[:, SSD_INNER + SSD_GROUPS * SSD_STATE + g * SSD_STATE:
                     SSD_INNER + SSD_GROUPS * SSD_STATE + (g + 1) * SSD_STATE]
        cb = lax.dot_general(cg, bg, (((1,), (1,)), ((), ())), preferred_element_type=F32)
        prev = state_scr[g]
        y = jnp.dot(cg, prev.astype(BF16), preferred_element_type=F32) * eacs[:, g * gw:(g + 1) * gw]
        xg = xdt[:, g * gw:(g + 1) * gw]
        for hh in range(hpg):
            h = g * hpg + hh
            seg = acs[:, h:h + 1] - acs_t[h:h + 1, :]
            dec = jnp.exp(jnp.where(causal, seg, -jnp.inf))
            xm = jnp.where(lane_g // SSD_HEADDIM == hh, xg, 0.0).astype(BF16)
            y = y + jnp.dot((cb * dec).astype(BF16), xm, preferred_element_type=F32)
        st = lax.dot_general(bg, x_end[:, g * gw:(g + 1) * gw], (((0,), (0,)), ((), ())),
                             preferred_element_type=F32)
        state_scr[g] = chunk_decay[:, g * gw:(g + 1) * gw] * prev + st
        ys.append(y)

    y = jnp.concatenate(ys, axis=1) + dfull_ref[...] * xs
    zf = z_ref[...].astype(F32)
    yz = y * (zf * _sigmoid(zf))
    outs = []
    for g in range(SSD_GROUPS):
        yg = yz[:, g * gw:(g + 1) * gw]
        outs.append(yg * lax.rsqrt(jnp.mean(yg * yg, axis=-1, keepdims=True) + EPS))
    o_ref[...] = (jnp.concatenate(outs, axis=1) * norm_ref[...]).astype(BF16)


def _ssd(xbc, dt, z, alog_p, d_full, ssd_norm, expand):
    b, s, _ = xbc.shape
    ln = SSD_L
    tok = lambda w: pl.BlockSpec((None, ln, w), lambda bi, c: (bi, c, 0))
    full = lambda a: pl.BlockSpec(a.shape, lambda bi, c: (0,) * a.ndim)
    return pl.pallas_call(
        _ssd_kernel,
        grid=(b, s // ln),
        in_specs=[tok(SSD_CONV_DIM), tok(LANES), tok(SSD_INNER), full(alog_p), full(d_full), full(ssd_norm),
                  full(expand)],
        out_specs=tok(SSD_INNER),
        out_shape=jax.ShapeDtypeStruct((b, s, SSD_INNER), BF16),
        scratch_shapes=[pltpu.VMEM((SSD_GROUPS, SSD_STATE, SSD_INNER // SSD_GROUPS), F32)],
        compiler_params=_cparams(("arbitrary", "arbitrary")),
        name="ssd",
    )(xbc, dt, z, alog_p, d_full, ssd_norm, expand)


def _memkv_kernel(mem_ref, wk_ref, wv_ref, k_ref, v_ref):
    m = mem_ref[...].astype(BF16)
    k_ref[...] = jnp.dot(m, wk_ref[...], preferred_element_type=F32).astype(BF16)
    v_ref[...] = jnp.dot(m, wv_ref[...], preferred_element_type=F32).astype(BF16)


def _memkv(mem, wk, wv):
    b, m, d = mem.shape
    blk = pl.BlockSpec((None, m, d), lambda bi: (bi, 0, 0))
    full = lambda a: pl.BlockSpec(a.shape, lambda bi: (0,) * a.ndim)
    return pl.pallas_call(
        _memkv_kernel,
        grid=(b,),
        in_specs=[blk, full(wk), full(wv)],
        out_specs=(blk, blk),
        out_shape=(jax.ShapeDtypeStruct((b, m, d), BF16), jax.ShapeDtypeStruct((b, m, d), BF16)),
        compiler_params=_cparams(("arbitrary",)),
        name="memkv",
    )(mem, wk, wv)


def _split_bf16(w):
    hi = w.astype(BF16)
    return hi, (w - hi.astype(F32)).astype(BF16)


def _router_logits(h, wrh_ref, wrl_ref, br_ref):
    h_hi, h_lo = _split_bf16(h)
    return (jnp.dot(h_hi, wrh_ref[...], preferred_element_type=F32)
            + jnp.dot(h_hi, wrl_ref[...], preferred_element_type=F32)
            + jnp.dot(h_lo, wrh_ref[...], preferred_element_type=F32)) + br_ref[...]


def _is_group_lane(lane):
    return (lane >= N_EXPERTS) & (lane < N_EXPERTS + N_EXPERT_GROUPS)


def _mid_kernel(x_ref, ol_ref, ss_ref, wmix_ref, g1_ref, b1_ref, wq_ref, km_ref, vm_ref, wo_ref,
                g2_ref, b2_ref, wrh_ref, wrl_ref, br_ref, row_ref, meta_ref, cnt_ref, run_scr):
    step = pl.program_id(0) * pl.num_programs(1) + pl.program_id(1)
    tm = x_ref.shape[0]
    n_lat = MLA_HEADS * MLA_KV_RANK

    @pl.when(step == 0)
    def _():
        run_scr[...] = jnp.zeros(run_scr.shape, F32)

    ns = MID_SUB
    lane = lax.broadcasted_iota(jnp.int32, (ns, LANES), 1)
    is_g = _is_group_lane(lane)
    r = lax.broadcasted_iota(jnp.int32, (ns, ns), 0)
    c = lax.broadcasted_iota(jnp.int32, (ns, ns), 1)
    strictly_before = (c < r).astype(BF16)
    run = run_scr[...]
    for sub in range(tm // ns):
        rs = slice(sub * ns, (sub + 1) * ns)
        mix = (jnp.dot(ol_ref[rs, :], wmix_ref[0:n_lat, :], preferred_element_type=F32)
               + jnp.dot(ss_ref[rs, :], wmix_ref[n_lat:, :], preferred_element_type=F32))
        h1 = _layer_norm(ALPHA * x_ref[rs, :] + mix, g1_ref[...], b1_ref[...])

        q = jnp.dot(h1.astype(BF16), wq_ref[...], preferred_element_type=F32).astype(BF16)
        outs = []
        for h in range(XA_HEADS):
            sl = slice(h * XA_HEAD_DIM, (h + 1) * XA_HEAD_DIM)
            s = lax.dot_general(q[:, sl], km_ref[:, sl], (((1,), (1,)), ((), ())), preferred_element_type=F32)
            p = jnp.exp(s - jnp.max(s, axis=-1, keepdims=True))
            o = jnp.dot(p.astype(BF16), vm_ref[:, sl], preferred_element_type=F32)
            outs.append((o / jnp.sum(p, axis=-1, keepdims=True)).astype(BF16))
        xa = jnp.dot(jnp.concatenate(outs, axis=1), wo_ref[...], preferred_element_type=F32)
        h2 = _layer_norm(ALPHA * h1 + xa, g2_ref[...], b2_ref[...])

        logits = _router_logits(h2, wrh_ref, wrl_ref, br_ref)
        gmax = jnp.max(jnp.where(is_g, logits, -jnp.inf), axis=-1, keepdims=True)
        g_idx = jnp.min(jnp.where(is_g & (logits == gmax), lane - N_EXPERTS, LANES), axis=-1, keepdims=True)
        onehot = (lane == g_idx).astype(BF16)
        before = jnp.dot(strictly_before, onehot, preferred_element_type=F32) + run
        rank = jnp.sum(jnp.where(lane == g_idx, before, 0.0), axis=-1, keepdims=True)
        run = run + jnp.sum(onehot.astype(F32), axis=0, keepdims=True)
        meta_ref[rs, :] = jnp.where(lane == 0, g_idx.astype(F32), jnp.where(lane == 1, rank, 0.0))

        for k in range(ROW_SUB):
            row_ref[pl.ds(sub * ns * ROW_SUB + k, ns, stride=ROW_SUB), :] = h2[:, k * LANES:(k + 1) * LANES]
    run_scr[...] = run
    cnt_ref[...] = run


def _mid(x, o_lat, ssd_out, w_mix, g1, b1, wq, kmem, vmem, wo, g2, b2, wrh, wrl, br):
    b, s, d = x.shape
    tm = TM_MID
    tok = lambda w: pl.BlockSpec((None, tm, w), lambda bi, j: (bi, j, 0))
    full = lambda a: pl.BlockSpec(a.shape, lambda bi, j: (0,) * a.ndim)
    mem = pl.BlockSpec((None,) + kmem.shape[1:], lambda bi, j: (bi, 0, 0))
    tok_rows = lambda n: pl.BlockSpec((None, n * ROW_SUB, LANES), lambda bi, j: (bi, j, 0))
    return pl.pallas_call(
        _mid_kernel,
        grid=(b, s // tm),
        in_specs=[tok(d), tok(o_lat.shape[-1]), tok(SSD_INNER), full(w_mix), full(g1), full(b1), full(wq),
                  mem, mem, full(wo), full(g2), full(b2), full(wrh), full(wrl), full(br)],
        out_specs=(tok_rows(tm), tok(LANES), pl.BlockSpec((1, LANES), lambda bi, j: (0, 0))),
        out_shape=(jax.ShapeDtypeStruct((b, s * ROW_SUB, LANES), F32),
                   jax.ShapeDtypeStruct((b, s, LANES), F32),
                   jax.ShapeDtypeStruct((1, LANES), F32)),
        scratch_shapes=[pltpu.VMEM((1, LANES), F32)],
        compiler_params=_cparams(("arbitrary", "arbitrary")),
        name="mid",
    )(x, o_lat, ssd_out, w_mix, g1, b1, wq, kmem, vmem, wo, g2, b2, wrh, wrl, br)


def _scatter_kernel(dest_ref, padrow_ref, rows_ref, out_hbm, zero_scr, sem, psem):
    c = pl.program_id(0)
    n = dest_ref.shape[1]

    def issue(t, carry):
        pltpu.make_async_copy(rows_ref.at[_row(t)], out_hbm.at[_row(dest_ref[0, t])], sem).start()
        return carry

    lax.fori_loop(0, n, issue, 0)

    npad = padrow_ref.shape[1]

    @pl.when(c == 0)
    def _():
        zero_scr[...] = jnp.zeros(zero_scr.shape, F32)

        def issue_pad(k, carry):
            pltpu.make_async_copy(zero_scr, out_hbm.at[_row(padrow_ref[0, k])], psem).start()
            return carry

        lax.fori_loop(0, npad, issue_pad, 0)

        def wait_pad(k, carry):
            pltpu.make_async_copy(zero_scr, out_hbm.at[_row(0)], psem).wait()
            return carry

        lax.fori_loop(0, npad, wait_pad, 0)

    pltpu.make_async_copy(rows_ref, out_hbm.at[pl.ds(0, n * ROW_SUB)], sem).wait()


def _scatter_rows(rows, dest3, padrow, n_out):
    t = rows.shape[0] // ROW_SUB
    n = dest3.shape[-1]
    return pl.pallas_call(
        _scatter_kernel,
        grid=(t // n,),
        in_specs=[pl.BlockSpec((None, 1, n), lambda c: (c, 0, 0), memory_space=pltpu.SMEM),
                  pl.BlockSpec(padrow.shape, lambda c: (0, 0), memory_space=pltpu.SMEM),
                  pl.BlockSpec((n * ROW_SUB, LANES), lambda c: (c, 0))],
        out_specs=pl.BlockSpec(memory_space=pl.ANY),
        out_shape=jax.ShapeDtypeStruct((n_out * ROW_SUB, LANES), rows.dtype),
        scratch_shapes=[pltpu.VMEM((ROW_SUB, LANES), F32), pltpu.SemaphoreType.DMA(()),
                        pltpu.SemaphoreType.DMA(())],
        compiler_params=_cparams(("arbitrary",)),
        name="scatter",
    )(dest3, padrow, rows)


def _unsort_kernel(dcur_ref, dnext_ref, rows_hbm, o_ref, buf, sem):
    c = pl.program_id(0)
    n = o_ref.shape[0]
    slot = c % 2

    def issue(dref, sl):
        def body(t, carry):
            pltpu.make_async_copy(rows_hbm.at[_row(dref[0, t])], buf.at[sl, _row(t)], sem.at[sl]).start()
            return carry

        lax.fori_loop(0, n, body, 0)

    @pl.when(c == 0)
    def _():
        issue(dcur_ref, 0)

    @pl.when(c + 1 < pl.num_programs(0))
    def _():
        issue(dnext_ref, 1 - slot)

    pltpu.make_async_copy(rows_hbm.at[pl.ds(0, n * ROW_SUB)], buf.at[slot], sem.at[slot]).wait()
    for k in range(ROW_SUB):
        o_ref[:, k * LANES:(k + 1) * LANES] = buf[slot, _lane_block(k, n), :]


def _unsort_rows(rows, dest3, n_tok):
    n = dest3.shape[-1]
    nc = n_tok // n
    return pl.pallas_call(
        _unsort_kernel,
        grid=(nc,),
        in_specs=[pl.BlockSpec((None, 1, n), lambda c: (c, 0, 0), memory_space=pltpu.SMEM),
                  pl.BlockSpec((None, 1, n), lambda c: (jnp.minimum(c + 1, nc - 1), 0, 0),
                               memory_space=pltpu.SMEM),
                  pl.BlockSpec(memory_space=pl.ANY)],
        out_specs=pl.BlockSpec((n, D_MODEL), lambda c: (c, 0)),
        out_shape=jax.ShapeDtypeStruct((n_tok, D_MODEL), rows.dtype),
        scratch_shapes=[pltpu.VMEM((2, n * ROW_SUB, LANES), F32), pltpu.SemaphoreType.DMA((2,))],
        compiler_params=_cparams(("arbitrary",)),
        name="unsort",
    )(dest3, dest3, rows)


def _moe_kernel(gid_ref, nused_ref, rows_ref, wrh_ref, wrl_ref, br_ref, wg_ref, wu_ref, wd_ref, g3_ref, b3_ref,
                o_ref):
    i = pl.program_id(0)
    tm = rows_ref.shape[0] // ROW_SUB

    @pl.when(i < nused_ref[0])
    def _():
        g = gid_ref[i]
        x = jnp.concatenate([rows_ref[_lane_block(k, tm), :] for k in range(ROW_SUB)], axis=1)

        logits = _router_logits(x, wrh_ref, wrl_ref, br_ref)
        lane = lax.broadcasted_iota(jnp.int32, (tm, LANES), 1)
        is_g = _is_group_lane(lane)
        gmax = jnp.max(jnp.where(is_g, logits, -jnp.inf), axis=-1, keepdims=True)
        gsel = jnp.sum(jnp.where(lane == N_EXPERTS + g, logits, 0.0), axis=-1, keepdims=True)
        g_gate = jnp.exp(gsel - gmax) / jnp.sum(jnp.where(is_g, jnp.exp(logits - gmax), 0.0),
                                                axis=-1, keepdims=True)
        in_grp = (lane < N_EXPERTS) & (lane // EXPERTS_PER_GROUP == g)
        t1 = jnp.max(jnp.where(in_grp, logits, -jnp.inf), axis=-1, keepdims=True)
        i1 = jnp.min(jnp.where(in_grp & (logits == t1), lane, LANES), axis=-1, keepdims=True)
        rest = in_grp & (lane != i1)
        t2 = jnp.max(jnp.where(rest, logits, -jnp.inf), axis=-1, keepdims=True)
        i2 = jnp.min(jnp.where(rest & (logits == t2), lane, LANES), axis=-1, keepdims=True)
        e21 = jnp.exp(t2 - t1)
        w1 = g_gate / (1.0 + e21)
        comb = jnp.where(lane == i1, w1, 0.0) + jnp.where(lane == i2, w1 * e21, 0.0)

        xb = x.astype(BF16)
        acc = jnp.zeros(x.shape, F32)
        for e in range(EXPERTS_PER_GROUP):
            ce = jnp.sum(jnp.where(lane == g * EXPERTS_PER_GROUP + e, comb, 0.0), axis=-1, keepdims=True)
            hg = jnp.dot(xb, wg_ref[e], preferred_element_type=F32)
            hu = jnp.dot(xb, wu_ref[e], preferred_element_type=F32)
            hd = hg * _sigmoid(hg) * hu * ce
            acc = acc + jnp.dot(hd.astype(BF16), wd_ref[e], preferred_element_type=F32)
        y = _layer_norm(ALPHA * x + acc, g3_ref[...], b3_ref[...])
        for k in range(ROW_SUB):
            o_ref[_lane_block(k, tm), :] = y[:, k * LANES:(k + 1) * LANES]

    @pl.when(i >= nused_ref[0])
    def _():
        o_ref[...] = jnp.zeros(o_ref.shape, F32)


def _moe(rows_sorted, tile_gid, n_used, wrh, wrl, br, wg, wu, wd, g3, b3):
    n_rows = rows_sorted.shape[0] // ROW_SUB
    tm = TM_MOE
    nt = n_rows // tm

    def tile_map(i, gid, nused):
        return (jnp.maximum(jnp.minimum(i, nused[0] - 1), 0), 0)

    def w_map(i, gid, nused):
        return (gid[i], 0, 0)

    full = lambda a: pl.BlockSpec(a.shape, lambda i, gid, nused: (0,) * a.ndim)
    e = EXPERTS_PER_GROUP
    grid_spec = pltpu.PrefetchScalarGridSpec(
        num_scalar_prefetch=2,
        grid=(nt,),
        in_specs=[pl.BlockSpec((tm * ROW_SUB, LANES), tile_map), full(wrh), full(wrl), full(br),
                  pl.BlockSpec((e, D_MODEL, D_EXPERT), w_map),
                  pl.BlockSpec((e, D_MODEL, D_EXPERT), w_map),
                  pl.BlockSpec((e, D_EXPERT, D_MODEL), w_map),
                  full(g3), full(b3)],
        out_specs=pl.BlockSpec((tm * ROW_SUB, LANES), lambda i, gid, nused: (i, 0)),
    )
    return pl.pallas_call(
        _moe_kernel,
        grid_spec=grid_spec,
        out_shape=jax.ShapeDtypeStruct((n_rows * ROW_SUB, LANES), F32),
        compiler_params=_cparams(("arbitrary",)),
        name="moe",
    )(tile_gid, n_used, rows_sorted, wrh, wrl, br, wg, wu, wd, g3, b3)


def _rope_tables(seq):
    pos = jnp.arange(seq, dtype=F32)
    inv_freq = ROPE_THETA ** (-jnp.arange(0, MLA_ROPE, 2, dtype=F32) / MLA_ROPE)
    ang = pos[:, None] * inv_freq[None, :]
    cos = jnp.tile(jnp.cos(ang), (1, LANES // (MLA_ROPE // 2)))
    sin = jnp.tile(jnp.sin(ang), (1, LANES // (MLA_ROPE // 2)))
    return cos, sin


def _pad_lanes(v, n=LANES):
    v = v.reshape(1, -1)
    return jnp.pad(v, ((0, 0), (0, n - v.shape[1])))


def kernel(x, mem, w_in, mla_q_norm, w_q_up, mla_kv_norm, w_kv_up, ssd_conv_w, ssd_conv_b, ssd_dt_bias, ssd_a_log, ssd_d, ssd_norm, w_out, ln1_g, ln1_b, xa_wq, xa_wk, xa_wv, xa_wo, ln2_g, ln2_b, router_group_w, router_group_b, router_expert_w, router_expert_b, expert_w_gate, expert_w_up, expert_w_down, ln3_g, ln3_b):
    b, s, d = x.shape
    t = b * s
    l = 0
    row = lambda v: v[l].reshape(1, -1)

    wi = w_in[l]
    o1 = MLA_Q_RANK
    o2 = o1 + MLA_KV_RANK
    o3 = o2 + MLA_ROPE
    o4 = o3 + SSD_INNER
    o5 = o4 + SSD_CONV_DIM
    w_kr = wi[:, o2:o3]
    half = MLA_ROPE // 2
    w_kr_rot = jnp.concatenate([-w_kr[:, half:], w_kr[:, :half]], axis=1)
    reps = LANES // MLA_ROPE
    w_in_r = jnp.concatenate(
        [wi[:, :o2], wi[:, o3:o4], wi[:, o4:o5], jnp.tile(w_kr, (1, reps)), jnp.tile(w_kr_rot, (1, reps)),
         wi[:, o5:], jnp.zeros((d, LANES - SSD_HEADS), F32)], axis=1).astype(BF16)
    cos4, sin4 = _rope_tables(s)
    attn_scale = (MLA_NOPE + MLA_ROPE) ** -0.5 * math.log2(math.e)
    w_abs, w_pe, w_per, w_mix = _fold_weights(w_q_up[l], w_kv_up[l], mla_kv_norm[l], w_out[l], attn_scale)
    expand = (jnp.arange(LANES)[:, None] == (jnp.arange(SSD_INNER)[None, :] // SSD_HEADDIM)).astype(BF16)
    d_full = jnp.repeat(ssd_d[l], SSD_HEADDIM).reshape(1, SSD_INNER)
    wrh, wrl = _split_bf16(jnp.concatenate(
        [router_expert_w[l], router_group_w[l], jnp.zeros((d, LANES - N_EXPERTS - N_EXPERT_GROUPS), F32)], axis=1))
    br = _pad_lanes(jnp.concatenate([router_expert_b[l], router_group_b[l]]))

    cq, kc, z, xbc, dt = _inproj(x, w_in_r, row(mla_q_norm), ssd_conv_w[l], row(ssd_conv_b),
                                 _pad_lanes(ssd_dt_bias[l]), cos4, sin4)
    o_lat = _attention(cq, kc, w_abs, w_pe, w_per, cos4, sin4)
    ssd_out = _ssd(xbc, dt, z, _pad_lanes(ssd_a_log[l]), d_full, row(ssd_norm), expand)

    kmem, vmem = _memkv(mem, xa_wk[l].astype(BF16), xa_wv[l].astype(BF16))
    wq = (xa_wq[l] * (XA_HEAD_DIM ** -0.5)).astype(BF16)
    rows, meta, counts = _mid(x, o_lat, ssd_out, w_mix, row(ln1_g), row(ln1_b), wq, kmem, vmem,
                              xa_wo[l].astype(BF16), row(ln2_g), row(ln2_b), wrh, wrl, br)

    tm = TM_MOE
    nt = t // tm + N_EXPERT_GROUPS - 1
    meta = meta.reshape(t, LANES)
    g_idx = meta[:, 0].astype(jnp.int32)
    rank = meta[:, 1].astype(jnp.int32)
    cnt = counts[0, :N_EXPERT_GROUPS].astype(jnp.int32)
    ntile = (cnt + tm - 1) // tm
    tile_end = jnp.cumsum(ntile)
    tile_start = tile_end - ntile
    dest = tile_start[g_idx] * tm + rank
    n_used = tile_end[-1:].astype(jnp.int32)
    tile_gid = jnp.minimum(jnp.sum(jnp.arange(nt)[:, None] >= tile_end[None, :], axis=1),
                           N_EXPERT_GROUPS - 1).astype(jnp.int32)
    pad_end = jnp.cumsum(ntile * tm - cnt)
    k = jnp.arange(nt * tm - t)
    kg = jnp.sum(k[:, None] >= pad_end[None, :], axis=1)
    kgc = jnp.minimum(kg, N_EXPERT_GROUPS - 1)
    off = k - jnp.where(kg > 0, pad_end[jnp.maximum(kg, 1) - 1], 0)
    padrow = jnp.where(kg < N_EXPERT_GROUPS, tile_start[kgc] * tm + cnt[kgc] + off,
                       tile_end[-1] * tm + off).astype(jnp.int32).reshape(1, -1)
    sc_chunk = min(SCATTER_CHUNK, t)
    un_chunk = min(UNSORT_CHUNK, t)

    rows_sorted = _scatter_rows(rows.reshape(t * ROW_SUB, LANES), dest.reshape(t // sc_chunk, 1, sc_chunk),
                                padrow, nt * tm)
    out_sorted = _moe(rows_sorted, tile_gid, n_used, wrh, wrl, br, expert_w_gate[l].astype(BF16),
                      expert_w_up[l].astype(BF16), expert_w_down[l].astype(BF16), row(ln3_g), row(ln3_b))
    out = _unsort_rows(out_sorted, dest.reshape(t // un_chunk, 1, un_chunk), t)
    return out.reshape(b, s, d)
```

```python
import functools
import math

import jax
import jax.numpy as jnp
from jax import lax
from jax.experimental import pallas as pl
from jax.experimental.pallas import tpu as pltpu

F32 = jnp.float32
BF16 = jnp.bfloat16
HIGHEST = lax.Precision.HIGHEST

EPS = 1e-5
NEG_INF = -1e30
CHUNK = 64
ROPE_THETA = 10000.0

D_MODEL = 1024
MLA_HEADS = 8
MLA_NOPE = 64
MLA_ROPE = 32
MLA_V = 64
MLA_Q_RANK = 256
MLA_KV_RANK = 128
SSD_HEADS = 8
SSD_HEADDIM = 64
SSD_INNER = 512
SSD_GROUPS = 2
SSD_STATE = 128
SSD_CONV = 4
SSD_CONV_DIM = 1024
XA_HEADS = 4
XA_HEAD_DIM = 256
N_EXPERT_GROUPS = 4
EXPERTS_PER_GROUP = 8
N_EXPERTS = 32
D_EXPERT = 256
DEPTH = 1
ALPHA = (2.0 * DEPTH) ** 0.25

LANES = 128
V7X_VMEM_LIMIT = 56 * 1024 * 1024

C_CQ = 0
C_CKV = 256
C_Z = 384
C_XBC = 896
C_KPE = 1920
C_KPER = 2048
C_DT = 2176
IN_COLS_R = 2304

TM_IN = 512
Q_BLK = 256
SSD_L = 256
TM_MID = 512
MID_SUB = 512
TM_MOE = 256
PAIRS_PER_GROUP = EXPERTS_PER_GROUP * (EXPERTS_PER_GROUP - 1) // 2
N_BUCKETS = LANES
SCATTER_CHUNK = 2048
UNSORT_CHUNK = 1024
ROW_SUB = D_MODEL // LANES


def _cparams(sem):
    return pltpu.CompilerParams(dimension_semantics=sem, vmem_limit_bytes=V7X_VMEM_LIMIT)


def _sigmoid(x):
    return 1.0 / (1.0 + jnp.exp(-x))


def _split3(x):
    hi = x.astype(BF16)
    r = x - hi.astype(F32)
    mid = r.astype(BF16)
    return hi, mid, (r - mid.astype(F32)).astype(BF16)


def _row(t):
    return pl.ds(pl.multiple_of(t * ROW_SUB, ROW_SUB), ROW_SUB)


def _lane_block(k, n):
    return pl.ds(k, n, stride=ROW_SUB)


def _layer_norm(x, g, b):
    mu = jnp.mean(x, axis=-1, keepdims=True)
    xc = x - mu
    var = jnp.mean(xc * xc, axis=-1, keepdims=True)
    return xc * lax.rsqrt(var + EPS) * g + b


def _fold_kernel(wqn_ref, wukt_ref, wuv_ref, woa_ref, gkv_row_ref, gkv_col_ref, wabs_ref, wof_ref, *, scale):
    for h in range(MLA_HEADS):
        wabs_ref[h] = scale * jnp.dot(wqn_ref[h], wukt_ref[h] * gkv_row_ref[...],
                                      precision=HIGHEST, preferred_element_type=F32)
        wof_ref[h] = jnp.dot(wuv_ref[h] * gkv_col_ref[...], woa_ref[h],
                             precision=HIGHEST, preferred_element_type=F32)


def _fold_weights(w_q_up, w_kv_up, mla_kv_norm, w_out, scale):
    wq = w_q_up.reshape(MLA_Q_RANK, MLA_HEADS, MLA_NOPE + MLA_ROPE)
    wkv = w_kv_up.reshape(MLA_KV_RANK, MLA_HEADS, MLA_NOPE + MLA_V)
    wqn = jnp.transpose(wq[:, :, :MLA_NOPE], (1, 0, 2))
    wukt = jnp.transpose(wkv[:, :, :MLA_NOPE], (1, 2, 0))
    wuv = jnp.transpose(wkv[:, :, MLA_NOPE:], (1, 0, 2))
    woa = w_out[:MLA_HEADS * MLA_V].reshape(MLA_HEADS, MLA_V, D_MODEL)
    wabs, wof = pl.pallas_call(
        functools.partial(_fold_kernel, scale=scale),
        out_shape=(jax.ShapeDtypeStruct((MLA_HEADS, MLA_Q_RANK, MLA_KV_RANK), F32),
                   jax.ShapeDtypeStruct((MLA_HEADS, MLA_KV_RANK, D_MODEL), F32)),
        name="fold",
    )(wqn, wukt, wuv, woa, mla_kv_norm.reshape(1, MLA_KV_RANK), mla_kv_norm.reshape(MLA_KV_RANK, 1))
    w_abs = jnp.transpose(wabs, (1, 0, 2)).reshape(MLA_Q_RANK, MLA_HEADS * MLA_KV_RANK).astype(BF16)
    w_pe3 = wq[:, :, MLA_NOPE:] * scale
    half = MLA_ROPE // 2
    w_per3 = jnp.concatenate([-w_pe3[:, :, half:], w_pe3[:, :, :half]], axis=-1)
    w_pe = w_pe3.reshape(MLA_Q_RANK, MLA_HEADS * MLA_ROPE).astype(BF16)
    w_per = w_per3.reshape(MLA_Q_RANK, MLA_HEADS * MLA_ROPE).astype(BF16)
    w_mix = jnp.concatenate([wof.reshape(MLA_HEADS * MLA_KV_RANK, D_MODEL), w_out[MLA_HEADS * MLA_V:]],
                            axis=0).astype(BF16)
    return w_abs, w_pe, w_per, w_mix


def _inproj_kernel(x_ref, w_ref, gq_ref, cw_ref, cb_ref, dtb_ref, cos_ref, sin_ref,
                   cq_ref, kc_ref, z_ref, xbc_ref, dt_ref, cbuf):
    j = pl.program_id(1)
    tm = x_ref.shape[0]
    @pl.when(j == 0)
    def _():
        cbuf[0:8, :] = jnp.zeros((8, SSD_CONV_DIM), F32)

    xb = x_ref[...].astype(BF16)
    seg = lambda lo, n: jnp.dot(xb, w_ref[:, lo:lo + n], preferred_element_type=F32)

    u = seg(C_XBC, SSD_CONV_DIM)
    cbuf[8:8 + tm, :] = u
    acc = cb_ref[...] + cw_ref[SSD_CONV - 1:SSD_CONV, :] * u
    for k in range(SSD_CONV - 1):
        acc = acc + cw_ref[k:k + 1, :] * cbuf[pl.ds(8 - (SSD_CONV - 1) + k, tm), :]
    xbc_ref[...] = (acc * _sigmoid(acc)).astype(BF16)
    cbuf[0:8, :] = cbuf[tm:tm + 8, :]

    z_ref[...] = seg(C_Z, SSD_INNER).astype(BF16)

    tail = seg(C_KPE, 3 * LANES)
    dtr = tail[:, 2 * LANES:] + dtb_ref[...]
    dt_ref[...] = jnp.maximum(dtr, 0.0) + jnp.log(1.0 + jnp.exp(-jnp.abs(dtr)))
    kpe = tail[:, :LANES] * cos_ref[...] + tail[:, LANES:2 * LANES] * sin_ref[...]

    lat = seg(C_CQ, MLA_Q_RANK + MLA_KV_RANK)
    c_q = lat[:, :MLA_Q_RANK]
    cq = c_q * lax.rsqrt(jnp.mean(c_q * c_q, axis=-1, keepdims=True) + EPS) * gq_ref[...]
    cq_ref[...] = cq.astype(BF16)
    c_kv = lat[:, MLA_Q_RANK:]
    ckv = c_kv * lax.rsqrt(jnp.mean(c_kv * c_kv, axis=-1, keepdims=True) + EPS)
    kc_ref[...] = jnp.concatenate([ckv, kpe], axis=1).astype(BF16)


def _inproj(x, w_in_r, gq, conv_w, conv_b, dt_bias_p, cos4, sin4):
    b, s, d = x.shape
    tm = TM_IN
    grid = (b, s // tm)
    tok = lambda w: pl.BlockSpec((None, tm, w), lambda bi, j: (bi, j, 0))
    full = lambda a: pl.BlockSpec(a.shape, lambda bi, j: (0,) * a.ndim)
    pos = pl.BlockSpec((tm, LANES), lambda bi, j: (j, 0))
    return pl.pallas_call(
        _inproj_kernel,
        grid=grid,
        in_specs=[tok(d), full(w_in_r), full(gq), full(conv_w), full(conv_b), full(dt_bias_p), pos, pos],
        out_specs=(tok(MLA_Q_RANK), tok(2 * LANES), tok(SSD_INNER), tok(SSD_CONV_DIM), tok(LANES)),
        out_shape=(jax.ShapeDtypeStruct((b, s, MLA_Q_RANK), BF16),
                   jax.ShapeDtypeStruct((b, s, 2 * LANES), BF16),
                   jax.ShapeDtypeStruct((b, s, SSD_INNER), BF16),
                   jax.ShapeDtypeStruct((b, s, SSD_CONV_DIM), BF16),
                   jax.ShapeDtypeStruct((b, s, LANES), F32)),
        scratch_shapes=[pltpu.VMEM((tm + 8, SSD_CONV_DIM), F32)],
        compiler_params=_cparams(("arbitrary", "arbitrary")),
        name="inproj",
    )(x, w_in_r, gq, conv_w, conv_b, dt_bias_p, cos4, sin4)


def _attn_kernel(cq_ref, kc_ref, wabs_ref, wpe_ref, wper_ref, cos_ref, sin_ref, o_ref,
                 q_scr, s_scr, mx_scr, ls_scr, acc_scr):
    i = pl.program_id(1)
    qb = cq_ref.shape[0]
    rows = MLA_HEADS * qb
    cq = cq_ref[...]
    qabs = jnp.dot(cq, wabs_ref[...], preferred_element_type=F32)
    cos8 = jnp.concatenate([cos_ref[...], cos_ref[...]], axis=1)
    sin8 = jnp.concatenate([sin_ref[...], sin_ref[...]], axis=1)
    qpe = (jnp.dot(cq, wpe_ref[...], preferred_element_type=F32) * cos8
           + jnp.dot(cq, wper_ref[...], preferred_element_type=F32) * sin8)
    lane = lax.broadcasted_iota(jnp.int32, (qb, LANES), 1)
    heads_per_blk = LANES // MLA_ROPE
    for h in range(MLA_HEADS):
        pe_blk = qpe[:, LANES * (h // heads_per_blk):LANES * (h // heads_per_blk + 1)]
        pe_h = jnp.where(lane // MLA_ROPE == h % heads_per_blk, pe_blk, 0.0)
        q_scr[h * qb:(h + 1) * qb, :] = jnp.concatenate(
            [qabs[:, h * MLA_KV_RANK:(h + 1) * MLA_KV_RANK], pe_h], axis=1).astype(BF16)

    def scores(j):
        kblk = kc_ref[pl.ds(pl.multiple_of(j * qb, qb), qb), :]
        return lax.dot_general(q_scr[...], kblk, (((1,), (1,)), ((), ())), preferred_element_type=F32)

    def lane_max(s):
        return jnp.maximum(s[:, :LANES], s[:, LANES:])

    mx_scr[...] = jnp.full((rows, LANES), -jnp.inf, F32)

    def pass1(j, carry):
        s = scores(j)
        s_scr[j] = s
        mx_scr[...] = jnp.maximum(mx_scr[...], lane_max(s))
        return carry

    lax.fori_loop(0, i, pass1, 0)
    r = lax.broadcasted_iota(jnp.int32, (qb, qb), 0)
    c = lax.broadcasted_iota(jnp.int32, (qb, qb), 1)
    visible = c // CHUNK <= r // CHUNK
    s = scores(i)
    s = jnp.concatenate([jnp.where(visible, s[h * qb:(h + 1) * qb], NEG_INF) for h in range(MLA_HEADS)],
                        axis=0)
    s_scr[i] = s
    m = jnp.max(jnp.maximum(mx_scr[...], lane_max(s)), axis=-1, keepdims=True)
    mx_scr[...] = jnp.broadcast_to(m, (rows, LANES))

    ls_scr[...] = jnp.zeros((rows, LANES), F32)
    acc_scr[...] = jnp.zeros((rows, MLA_KV_RANK), F32)

    def pass2(j, carry):
        sj = s_scr[j]
        mb = mx_scr[...]
        p = jnp.concatenate([jnp.exp2(sj[:, :LANES] - mb), jnp.exp2(sj[:, LANES:] - mb)], axis=1)
        ls_scr[...] = ls_scr[...] + p[:, :LANES] + p[:, LANES:]
        vblk = kc_ref[pl.ds(pl.multiple_of(j * qb, qb), qb), 0:MLA_KV_RANK]
        acc_scr[...] = acc_scr[...] + jnp.dot(p.astype(BF16), vblk, preferred_element_type=F32)
        return carry

    lax.fori_loop(0, i + 1, pass2, 0)

    o = acc_scr[...] * (1.0 / jnp.sum(ls_scr[...], axis=-1, keepdims=True))
    for h in range(MLA_HEADS):
        o_ref[:, h * MLA_KV_RANK:(h + 1) * MLA_KV_RANK] = o[h * qb:(h + 1) * qb, :].astype(BF16)


def _attention(cq, kc, w_abs, w_pe, w_per, cos4, sin4):
    b, s, _ = cq.shape
    qb = Q_BLK
    rows = MLA_HEADS * qb
    full = lambda a: pl.BlockSpec(a.shape, lambda bi, i: (0,) * a.ndim)
    pos = pl.BlockSpec((qb, LANES), lambda bi, i: (i, 0))
    return pl.pallas_call(
        _attn_kernel,
        grid=(b, s // qb),
        in_specs=[pl.BlockSpec((None, qb, MLA_Q_RANK), lambda bi, i: (bi, i, 0)),
                  pl.BlockSpec((None, s, 2 * LANES), lambda bi, i: (bi, 0, 0)),
                  full(w_abs), full(w_pe), full(w_per), pos, pos],
        out_specs=pl.BlockSpec((None, qb, MLA_HEADS * MLA_KV_RANK), lambda bi, i: (bi, i, 0)),
        out_shape=jax.ShapeDtypeStruct((b, s, MLA_HEADS * MLA_KV_RANK), BF16),
        scratch_shapes=[pltpu.VMEM((rows, 2 * LANES), BF16),
                        pltpu.VMEM((s // qb, rows, qb), F32),
                        pltpu.VMEM((rows, LANES), F32),
                        pltpu.VMEM((rows, LANES), F32),
                        pltpu.VMEM((rows, MLA_KV_RANK), F32)],
        compiler_params=_cparams(("arbitrary", "arbitrary")),
        name="attn",
    )(cq, kc, w_abs, w_pe, w_per, cos4, sin4)


def _ssd_kernel(xbc_ref, dt_ref, z_ref, alog_ref, dfull_ref, norm_ref, e_ref, o_ref, state_scr):
    c = pl.program_id(1)
    ln = xbc_ref.shape[0]
    gw = SSD_INNER // SSD_GROUPS
    hpg = SSD_HEADS // SSD_GROUPS

    @pl.when(c == 0)
    def _():
        state_scr[...] = jnp.zeros(state_scr.shape, F32)

    xs = xbc_ref[:, 0:SSD_INNER].astype(F32)
    dt = dt_ref[...]
    lane1 = lax.broadcasted_iota(jnp.int32, (1, LANES), 1)
    a = jnp.where(lane1 < SSD_HEADS, -jnp.exp(alog_ref[...]), 0.0)
    adt = dt * a
    row = lax.broadcasted_iota(jnp.int32, (ln, ln), 0)
    col = lax.broadcasted_iota(jnp.int32, (ln, ln), 1)
    causal = col <= row
    tril = causal.astype(BF16)
    e = e_ref[...]
    acs = sum(jnp.dot(tril, p, preferred_element_type=F32) for p in _split3(adt))
    acs_e = sum(jnp.dot(p, e, preferred_element_type=F32) for p in _split3(acs))
    dt_e = sum(jnp.dot(p, e, preferred_element_type=F32) for p in _split3(dt))
    acs_end = acs_e[ln - 1:ln, :]
    xdt = xs * dt_e
    x_end = (xdt * jnp.exp(acs_end - acs_e)).astype(BF16)
    eacs = jnp.exp(acs_e)
    chunk_decay = jnp.exp(acs_end)
    acs_t = acs.T
    lane_g = lax.broadcasted_iota(jnp.int32, (ln, gw), 1)

    ys = []
    for g in range(SSD_GROUPS):
        bg = xbc_ref[:, SSD_INNER + g * SSD_STATE:SSD_INNER + (g + 1) * SSD_STATE]
        cg = xbc_ref[:, SSD_INNER + SSD_GROUPS * SSD_STATE + g * SSD_STATE:
                     SSD_INNER + SSD_GROUPS * SSD_STATE + (g + 1) * SSD_STATE]
        cb = lax.dot_general(cg, bg, (((1,), (1,)), ((), ())), preferred_element_type=F32)
        prev = state_scr[g]
        y = jnp.dot(cg, prev.astype(BF16), preferred_element_type=F32) * eacs[:, g * gw:(g + 1) * gw]
        xg = xdt[:, g * gw:(g + 1) * gw]
        for hh in range(hpg):
            h = g * hpg + hh
            seg = acs[:, h:h + 1] - acs_t[h:h + 1, :]
            dec = jnp.exp(jnp.where(causal, seg, -jnp.inf))
            xm = jnp.where(lane_g // SSD_HEADDIM == hh, xg, 0.0).astype(BF16)
            y = y + jnp.dot((cb * dec).astype(BF16), xm, preferred_element_type=F32)
        st = lax.dot_general(bg, x_end[:, g * gw:(g + 1) * gw], (((0,), (0,)), ((), ())),
                             preferred_element_type=F32)
        state_scr[g] = chunk_decay[:, g * gw:(g + 1) * gw] * prev + st
        ys.append(y)

    y = jnp.concatenate(ys, axis=1) + dfull_ref[...] * xs
    zf = z_ref[...].astype(F32)
    yz = y * (zf * _sigmoid(zf))
    outs = []
    for g in range(SSD_GROUPS):
        yg = yz[:, g * gw:(g + 1) * gw]
        outs.append(yg * lax.rsqrt(jnp.mean(yg * yg, axis=-1, keepdims=True) + EPS))
    o_ref[...] = (jnp.concatenate(outs, axis=1) * norm_ref[...]).astype(BF16)


def _ssd(xbc, dt, z, alog_p, d_full, ssd_norm, expand):
    b, s, _ = xbc.shape
    ln = SSD_L
    tok = lambda w: pl.BlockSpec((None, ln, w), lambda bi, c: (bi, c, 0))
    full = lambda a: pl.BlockSpec(a.shape, lambda bi, c: (0,) * a.ndim)
    return pl.pallas_call(
        _ssd_kernel,
        grid=(b, s // ln),
        in_specs=[tok(SSD_CONV_DIM), tok(LANES), tok(SSD_INNER), full(alog_p), full(d_full), full(ssd_norm),
                  full(expand)],
        out_specs=tok(SSD_INNER),
        out_shape=jax.ShapeDtypeStruct((b, s, SSD_INNER), BF16),
        scratch_shapes=[pltpu.VMEM((SSD_GROUPS, SSD_STATE, SSD_INNER // SSD_GROUPS), F32)],
        compiler_params=_cparams(("arbitrary", "arbitrary")),
        name="ssd",
    )(xbc, dt, z, alog_p, d_full, ssd_norm, expand)


def _memkv_kernel(mem_ref, wk_ref, wv_ref, k_ref, v_ref):
    m = mem_ref[...].astype(BF16)
    k_ref[...] = jnp.dot(m, wk_ref[...], preferred_element_type=F32).astype(BF16)
    v_ref[...] = jnp.dot(m, wv_ref[...], preferred_element_type=F32).astype(BF16)


def _memkv(mem, wk, wv):
    b, m, d = mem.shape
    blk = pl.BlockSpec((None, m, d), lambda bi: (bi, 0, 0))
    full = lambda a: pl.BlockSpec(a.shape, lambda bi: (0,) * a.ndim)
    return pl.pallas_call(
        _memkv_kernel,
        grid=(b,),
        in_specs=[blk, full(wk), full(wv)],
        out_specs=(blk, blk),
        out_shape=(jax.ShapeDtypeStruct((b, m, d), BF16), jax.ShapeDtypeStruct((b, m, d), BF16)),
        compiler_params=_cparams(("arbitrary",)),
        name="memkv",
    )(mem, wk, wv)


def _split_bf16(w):
    hi = w.astype(BF16)
    return hi, (w - hi.astype(F32)).astype(BF16)


def _router_logits(h, wrh_ref, wrl_ref, br_ref):
    h_hi, h_lo = _split_bf16(h)
    return (jnp.dot(h_hi, wrh_ref[...], preferred_element_type=F32)
            + jnp.dot(h_hi, wrl_ref[...], preferred_element_type=F32)
            + jnp.dot(h_lo, wrh_ref[...], preferred_element_type=F32)) + br_ref[...]


def _is_group_lane(lane):
    return (lane >= N_EXPERTS) & (lane < N_EXPERTS + N_EXPERT_GROUPS)


def _mid_kernel(x_ref, ol_ref, ss_ref, wmix_ref, g1_ref, b1_ref, wq_ref, km_ref, vm_ref, wo_ref,
                g2_ref, b2_ref, wrh_ref, wrl_ref, br_ref, row_ref, meta_ref, cnt_ref, run_scr):
    step = pl.program_id(0) * pl.num_programs(1) + pl.program_id(1)
    tm = x_ref.shape[0]
    n_lat = MLA_HEADS * MLA_KV_RANK

    @pl.when(step == 0)
    def _():
        run_scr[...] = jnp.zeros(run_scr.shape, F32)

    ns = MID_SUB
    lane = lax.broadcasted_iota(jnp.int32, (ns, LANES), 1)
    is_g = _is_group_lane(lane)
    r = lax.broadcasted_iota(jnp.int32, (ns, ns), 0)
    c = lax.broadcasted_iota(jnp.int32, (ns, ns), 1)
    strictly_before = (c < r).astype(BF16)
    run = run_scr[...]
    for sub in range(tm // ns):
        rs = slice(sub * ns, (sub + 1) * ns)
        mix = (jnp.dot(ol_ref[rs, :], wmix_ref[0:n_lat, :], preferred_element_type=F32)
               + jnp.dot(ss_ref[rs, :], wmix_ref[n_lat:, :], preferred_element_type=F32))
        h1 = _layer_norm(ALPHA * x_ref[rs, :] + mix, g1_ref[...], b1_ref[...])

        q = jnp.dot(h1.astype(BF16), wq_ref[...], preferred_element_type=F32).astype(BF16)
        outs = []
        for h in range(XA_HEADS):
            sl = slice(h * XA_HEAD_DIM, (h + 1) * XA_HEAD_DIM)
            s = lax.dot_general(q[:, sl], km_ref[:, sl], (((1,), (1,)), ((), ())), preferred_element_type=F32)
            p = jnp.exp(s - jnp.max(s, axis=-1, keepdims=True))
            o = jnp.dot(p.astype(BF16), vm_ref[:, sl], preferred_element_type=F32)
            outs.append((o / jnp.sum(p, axis=-1, keepdims=True)).astype(BF16))
        xa = jnp.dot(jnp.concatenate(outs, axis=1), wo_ref[...], preferred_element_type=F32)
        h2 = _layer_norm(ALPHA * h1 + xa, g2_ref[...], b2_ref[...])

        logits = _router_logits(h2, wrh_ref, wrl_ref, br_ref)
        gmax = jnp.max(jnp.where(is_g, logits, -jnp.inf), axis=-1, keepdims=True)
        g_idx = jnp.min(jnp.where(is_g & (logits == gmax), lane - N_EXPERTS, LANES), axis=-1, keepdims=True)
        in_grp = (lane < N_EXPERTS) & (lane // EXPERTS_PER_GROUP == g_idx)
        t1 = jnp.max(jnp.where(in_grp, logits, -jnp.inf), axis=-1, keepdims=True)
        i1 = jnp.min(jnp.where(in_grp & (logits == t1), lane, LANES), axis=-1, keepdims=True)
        rest = in_grp & (lane != i1)
        t2 = jnp.max(jnp.where(rest, logits, -jnp.inf), axis=-1, keepdims=True)
        i2 = jnp.min(jnp.where(rest & (logits == t2), lane, LANES), axis=-1, keepdims=True)
        ea = jnp.minimum(i1, i2) - g_idx * EXPERTS_PER_GROUP
        eb = jnp.maximum(i1, i2) - g_idx * EXPERTS_PER_GROUP
        pair = ea * EXPERTS_PER_GROUP - (ea * (ea + 1)) // 2 + eb - ea - 1
        bucket = g_idx * PAIRS_PER_GROUP + pair
        onehot = (lane == bucket).astype(BF16)
        before = jnp.dot(strictly_before, onehot, preferred_element_type=F32) + run
        rank = jnp.sum(jnp.where(lane == bucket, before, 0.0), axis=-1, keepdims=True)
        run = run + jnp.sum(onehot.astype(F32), axis=0, keepdims=True)
        meta_ref[rs, :] = jnp.where(lane == 0, bucket.astype(F32), jnp.where(lane == 1, rank, 0.0))

        for k in range(ROW_SUB):
            row_ref[pl.ds(sub * ns * ROW_SUB + k, ns, stride=ROW_SUB), :] = h2[:, k * LANES:(k + 1) * LANES]
    run_scr[...] = run
    cnt_ref[...] = run


def _mid(x, o_lat, ssd_out, w_mix, g1, b1, wq, kmem, vmem, wo, g2, b2, wrh, wrl, br):
    b, s, d = x.shape
    tm = TM_MID
    tok = lambda w: pl.BlockSpec((None, tm, w), lambda bi, j: (bi, j, 0))
    full = lambda a: pl.BlockSpec(a.shape, lambda bi, j: (0,) * a.ndim)
    mem = pl.BlockSpec((None,) + kmem.shape[1:], lambda bi, j: (bi, 0, 0))
    tok_rows = lambda n: pl.BlockSpec((None, n * ROW_SUB, LANES), lambda bi, j: (bi, j, 0))
    return pl.pallas_call(
        _mid_kernel,
        grid=(b, s // tm),
        in_specs=[tok(d), tok(o_lat.shape[-1]), tok(SSD_INNER), full(w_mix), full(g1), full(b1), full(wq),
                  mem, mem, full(wo), full(g2), full(b2), full(wrh), full(wrl), full(br)],
        out_specs=(tok_rows(tm), tok(LANES), pl.BlockSpec((1, LANES), lambda bi, j: (0, 0))),
        out_shape=(jax.ShapeDtypeStruct((b, s * ROW_SUB, LANES), F32),
                   jax.ShapeDtypeStruct((b, s, LANES), F32),
                   jax.ShapeDtypeStruct((1, LANES), F32)),
        scratch_shapes=[pltpu.VMEM((1, LANES), F32)],
        compiler_params=_cparams(("arbitrary", "arbitrary")),
        name="mid",
    )(x, o_lat, ssd_out, w_mix, g1, b1, wq, kmem, vmem, wo, g2, b2, wrh, wrl, br)


def _scatter_kernel(dest_ref, ztile_ref, rows_ref, out_hbm, zero_scr, sem, zsem):
    c = pl.program_id(0)
    n = dest_ref.shape[1]
    tile_rows = zero_scr.shape[0]

    @pl.when(c == 0)
    def _():
        zero_scr[...] = jnp.zeros(zero_scr.shape, F32)

        def zero_copy(k):
            start = pl.multiple_of(jnp.maximum(ztile_ref[0, k], 0) * tile_rows, tile_rows)
            return pltpu.make_async_copy(zero_scr, out_hbm.at[pl.ds(start, tile_rows)], zsem)

        def issue_zero(k, carry):
            @pl.when(ztile_ref[0, k] >= 0)
            def _():
                zero_copy(k).start()
            return carry

        def wait_zero(k, carry):
            @pl.when(ztile_ref[0, k] >= 0)
            def _():
                zero_copy(k).wait()
            return carry

        lax.fori_loop(0, ztile_ref.shape[1], issue_zero, 0)
        lax.fori_loop(0, ztile_ref.shape[1], wait_zero, 0)

    def issue(t, carry):
        pltpu.make_async_copy(rows_ref.at[_row(t)], out_hbm.at[_row(dest_ref[0, t])], sem).start()
        return carry

    lax.fori_loop(0, n, issue, 0)
    pltpu.make_async_copy(rows_ref, out_hbm.at[pl.ds(0, n * ROW_SUB)], sem).wait()


def _scatter_rows(rows, dest3, ztile, n_out, tile):
    t = rows.shape[0] // ROW_SUB
    n = dest3.shape[-1]
    return pl.pallas_call(
        _scatter_kernel,
        grid=(t // n,),
        in_specs=[pl.BlockSpec((None, 1, n), lambda c: (c, 0, 0), memory_space=pltpu.SMEM),
                  pl.BlockSpec(ztile.shape, lambda c: (0, 0), memory_space=pltpu.SMEM),
                  pl.BlockSpec((n * ROW_SUB, LANES), lambda c: (c, 0))],
        out_specs=pl.BlockSpec(memory_space=pl.ANY),
        out_shape=jax.ShapeDtypeStruct((n_out * ROW_SUB, LANES), rows.dtype),
        scratch_shapes=[pltpu.VMEM((tile * ROW_SUB, LANES), F32), pltpu.SemaphoreType.DMA(()),
                        pltpu.SemaphoreType.DMA(())],
        compiler_params=_cparams(("arbitrary",)),
        name="scatter",
    )(dest3, ztile, rows)


def _unsort_kernel(dcur_ref, dnext_ref, rows_hbm, o_ref, buf, sem):
    c = pl.program_id(0)
    n = o_ref.shape[0]
    slot = c % 2

    def issue(dref, sl):
        def body(t, carry):
            pltpu.make_async_copy(rows_hbm.at[_row(dref[0, t])], buf.at[sl, _row(t)], sem.at[sl]).start()
            return carry

        lax.fori_loop(0, n, body, 0)

    @pl.when(c == 0)
    def _():
        issue(dcur_ref, 0)

    @pl.when(c + 1 < pl.num_programs(0))
    def _():
        issue(dnext_ref, 1 - slot)

    pltpu.make_async_copy(rows_hbm.at[pl.ds(0, n * ROW_SUB)], buf.at[slot], sem.at[slot]).wait()
    for k in range(ROW_SUB):
        o_ref[:, k * LANES:(k + 1) * LANES] = buf[slot, _lane_block(k, n), :]


def _unsort_rows(rows, dest3, n_tok):
    n = dest3.shape[-1]
    nc = n_tok // n
    return pl.pallas_call(
        _unsort_kernel,
        grid=(nc,),
        in_specs=[pl.BlockSpec((None, 1, n), lambda c: (c, 0, 0), memory_space=pltpu.SMEM),
                  pl.BlockSpec((None, 1, n), lambda c: (jnp.minimum(c + 1, nc - 1), 0, 0),
                               memory_space=pltpu.SMEM),
                  pl.BlockSpec(memory_space=pl.ANY)],
        out_specs=pl.BlockSpec((n, D_MODEL), lambda c: (c, 0)),
        out_shape=jax.ShapeDtypeStruct((n_tok, D_MODEL), rows.dtype),
        scratch_shapes=[pltpu.VMEM((2, n * ROW_SUB, LANES), F32), pltpu.SemaphoreType.DMA((2,))],
        compiler_params=_cparams(("arbitrary",)),
        name="unsort",
    )(dest3, dest3, rows)


def _moe_kernel(e1_ref, e2_ref, nused_ref, rows_ref, wr_ref, br_ref, wg1_ref, wu1_ref, wd1_ref,
                wg2_ref, wu2_ref, wd2_ref, g3_ref, b3_ref, o_ref):
    i = pl.program_id(0)
    tm = rows_ref.shape[0] // ROW_SUB

    @pl.when(i < nused_ref[0])
    def _():
        e1 = e1_ref[i]
        e2 = e2_ref[i]
        g = e1 // EXPERTS_PER_GROUP
        x = jnp.concatenate([rows_ref[_lane_block(k, tm), :] for k in range(ROW_SUB)], axis=1)
        xb = x.astype(BF16)

        logits = jnp.dot(xb, wr_ref[...], preferred_element_type=F32) + br_ref[...]
        lane = lax.broadcasted_iota(jnp.int32, (tm, LANES), 1)
        is_g = _is_group_lane(lane)
        pick = lambda idx: jnp.sum(jnp.where(lane == idx, logits, 0.0), axis=-1, keepdims=True)
        gmax = jnp.max(jnp.where(is_g, logits, -jnp.inf), axis=-1, keepdims=True)
        g_gate = jnp.exp(pick(N_EXPERTS + g) - gmax) / jnp.sum(jnp.where(is_g, jnp.exp(logits - gmax), 0.0),
                                                                axis=-1, keepdims=True)
        t1 = pick(e1)
        t2 = pick(e2)
        c1 = g_gate / (1.0 + jnp.exp(t2 - t1))
        c2 = g_gate / (1.0 + jnp.exp(t1 - t2))

        def expert(wg_ref, wu_ref, wd_ref, ce):
            hg = jnp.dot(xb, wg_ref[...], preferred_element_type=F32)
            hu = jnp.dot(xb, wu_ref[...], preferred_element_type=F32)
            hd = hg * _sigmoid(hg) * hu * ce
            return jnp.dot(hd.astype(BF16), wd_ref[...], preferred_element_type=F32)

        acc = expert(wg1_ref, wu1_ref, wd1_ref, c1) + expert(wg2_ref, wu2_ref, wd2_ref, c2)
        y = _layer_norm(ALPHA * x + acc, g3_ref[...], b3_ref[...])
        for k in range(ROW_SUB):
            o_ref[_lane_block(k, tm), :] = y[:, k * LANES:(k + 1) * LANES]

    @pl.when(i >= nused_ref[0])
    def _():
        o_ref[...] = jnp.zeros(o_ref.shape, F32)


def _moe(rows_sorted, tile_e1, tile_e2, n_used, wr, br, wg, wu, wd, g3, b3):
    n_rows = rows_sorted.shape[0] // ROW_SUB
    tm = TM_MOE
    nt = n_rows // tm

    def tile_map(i, e1, e2, nused):
        return (jnp.maximum(jnp.minimum(i, nused[0] - 1), 0), 0)

    first = lambda i, e1, e2, nused: (e1[i], 0, 0)
    second = lambda i, e1, e2, nused: (e2[i], 0, 0)
    full = lambda a: pl.BlockSpec(a.shape, lambda i, e1, e2, nused: (0,) * a.ndim)
    w_in = lambda m: pl.BlockSpec((None, D_MODEL, D_EXPERT), m)
    w_out = lambda m: pl.BlockSpec((None, D_EXPERT, D_MODEL), m)
    grid_spec = pltpu.PrefetchScalarGridSpec(
        num_scalar_prefetch=3,
        grid=(nt,),
        in_specs=[pl.BlockSpec((tm * ROW_SUB, LANES), tile_map), full(wr), full(br),
                  w_in(first), w_in(first), w_out(first), w_in(second), w_in(second), w_out(second),
                  full(g3), full(b3)],
        out_specs=pl.BlockSpec((tm * ROW_SUB, LANES), lambda i, e1, e2, nused: (i, 0)),
    )
    return pl.pallas_call(
        _moe_kernel,
        grid_spec=grid_spec,
        out_shape=jax.ShapeDtypeStruct((n_rows * ROW_SUB, LANES), F32),
        compiler_params=_cparams(("arbitrary",)),
        name="moe",
    )(tile_e1, tile_e2, n_used, rows_sorted, wr, br, wg, wu, wd, wg, wu, wd, g3, b3)


def _rope_tables(seq):
    pos = jnp.arange(seq, dtype=F32)
    inv_freq = ROPE_THETA ** (-jnp.arange(0, MLA_ROPE, 2, dtype=F32) / MLA_ROPE)
    ang = pos[:, None] * inv_freq[None, :]
    cos = jnp.tile(jnp.cos(ang), (1, LANES // (MLA_ROPE // 2)))
    sin = jnp.tile(jnp.sin(ang), (1, LANES // (MLA_ROPE // 2)))
    return cos, sin


def _pad_lanes(v, n=LANES):
    v = v.reshape(1, -1)
    return jnp.pad(v, ((0, 0), (0, n - v.shape[1])))


def kernel(x, mem, w_in, mla_q_norm, w_q_up, mla_kv_norm, w_kv_up, ssd_conv_w, ssd_conv_b, ssd_dt_bias, ssd_a_log, ssd_d, ssd_norm, w_out, ln1_g, ln1_b, xa_wq, xa_wk, xa_wv, xa_wo, ln2_g, ln2_b, router_group_w, router_group_b, router_expert_w, router_expert_b, expert_w_gate, expert_w_up, expert_w_down, ln3_g, ln3_b):
    b, s, d = x.shape
    t = b * s
    l = 0
    row = lambda v: v[l].reshape(1, -1)

    wi = w_in[l]
    o1 = MLA_Q_RANK
    o2 = o1 + MLA_KV_RANK
    o3 = o2 + MLA_ROPE
    o4 = o3 + SSD_INNER
    o5 = o4 + SSD_CONV_DIM
    w_kr = wi[:, o2:o3]
    half = MLA_ROPE // 2
    w_kr_rot = jnp.concatenate([-w_kr[:, half:], w_kr[:, :half]], axis=1)
    reps = LANES // MLA_ROPE
    w_in_r = jnp.concatenate(
        [wi[:, :o2], wi[:, o3:o4], wi[:, o4:o5], jnp.tile(w_kr, (1, reps)), jnp.tile(w_kr_rot, (1, reps)),
         wi[:, o5:], jnp.zeros((d, LANES - SSD_HEADS), F32)], axis=1).astype(BF16)
    cos4, sin4 = _rope_tables(s)
    attn_scale = (MLA_NOPE + MLA_ROPE) ** -0.5 * math.log2(math.e)
    w_abs, w_pe, w_per, w_mix = _fold_weights(w_q_up[l], w_kv_up[l], mla_kv_norm[l], w_out[l], attn_scale)
    expand = (jnp.arange(LANES)[:, None] == (jnp.arange(SSD_INNER)[None, :] // SSD_HEADDIM)).astype(BF16)
    d_full = jnp.repeat(ssd_d[l], SSD_HEADDIM).reshape(1, SSD_INNER)
    wrh, wrl = _split_bf16(jnp.concatenate(
        [router_expert_w[l], router_group_w[l], jnp.zeros((d, LANES - N_EXPERTS - N_EXPERT_GROUPS), F32)], axis=1))
    br = _pad_lanes(jnp.concatenate([router_expert_b[l], router_group_b[l]]))

    cq, kc, z, xbc, dt = _inproj(x, w_in_r, row(mla_q_norm), ssd_conv_w[l], row(ssd_conv_b),
                                 _pad_lanes(ssd_dt_bias[l]), cos4, sin4)
    o_lat = _attention(cq, kc, w_abs, w_pe, w_per, cos4, sin4)
    ssd_out = _ssd(xbc, dt, z, _pad_lanes(ssd_a_log[l]), d_full, row(ssd_norm), expand)

    kmem, vmem = _memkv(mem, xa_wk[l].astype(BF16), xa_wv[l].astype(BF16))
    wq = (xa_wq[l] * (XA_HEAD_DIM ** -0.5)).astype(BF16)
    rows, meta, counts = _mid(x, o_lat, ssd_out, w_mix, row(ln1_g), row(ln1_b), wq, kmem, vmem,
                              xa_wo[l].astype(BF16), row(ln2_g), row(ln2_b), wrh, wrl, br)

    tm = TM_MOE
    nt = t // tm + N_EXPERT_GROUPS * PAIRS_PER_GROUP - 1
    meta = meta.reshape(t, LANES)
    bucket = meta[:, 0].astype(jnp.int32)
    rank = meta[:, 1].astype(jnp.int32)
    cnt = counts[0].astype(jnp.int32)
    ntile = (cnt + tm - 1) // tm
    tile_end = jnp.cumsum(ntile)
    tile_start = tile_end - ntile
    dest = tile_start[bucket] * tm + rank
    n_used = tile_end[-1:].astype(jnp.int32)
    tile_bkt = jnp.minimum(jnp.sum(jnp.arange(nt)[:, None] >= tile_end[None, :], axis=1), N_BUCKETS - 1)
    pa, pb = zip(*[(a, c) for a in range(EXPERTS_PER_GROUP) for c in range(a + 1, EXPERTS_PER_GROUP)])
    bkt_g = jnp.minimum(tile_bkt // PAIRS_PER_GROUP, N_EXPERT_GROUPS - 1)
    tile_e1 = (bkt_g * EXPERTS_PER_GROUP + jnp.array(pa)[tile_bkt % PAIRS_PER_GROUP]).astype(jnp.int32)
    tile_e2 = (bkt_g * EXPERTS_PER_GROUP + jnp.array(pb)[tile_bkt % PAIRS_PER_GROUP]).astype(jnp.int32)
    spare = n_used[0] + jnp.arange(nt - t // tm)
    ztile = jnp.concatenate([jnp.where(ntile > 0, tile_end - 1, -1),
                             jnp.where(spare < nt, spare, -1)]).astype(jnp.int32).reshape(1, -1)
    sc_chunk = min(SCATTER_CHUNK, t)
    un_chunk = min(UNSORT_CHUNK, t)

    rows_sorted = _scatter_rows(rows.reshape(t * ROW_SUB, LANES), dest.reshape(t // sc_chunk, 1, sc_chunk),
                                ztile, nt * tm, tm)
    out_sorted = _moe(rows_sorted, tile_e1, tile_e2, n_used, wrh, br, expert_w_gate[l].astype(BF16),
                      expert_w_up[l].astype(BF16), expert_w_down[l].astype(BF16), row(ln3_g), row(ln3_b))
    out = _unsort_rows(out_sorted, dest.reshape(t // un_chunk, 1, un_chunk), t)
    return out.reshape(b, s, d)
```

```python
import functools
import math

import jax
import jax.numpy as jnp
from jax import lax
from jax.experimental import pallas as pl
from jax.experimental.pallas import tpu as pltpu

F32 = jnp.float32
BF16 = jnp.bfloat16
HIGHEST = lax.Precision.HIGHEST

EPS = 1e-5
NEG_INF = -1e30
CHUNK = 64
ROPE_THETA = 10000.0

D_MODEL = 1024
MLA_HEADS = 8
MLA_NOPE = 64
MLA_ROPE = 32
MLA_V = 64
MLA_Q_RANK = 256
MLA_KV_RANK = 128
SSD_HEADS = 8
SSD_HEADDIM = 64
SSD_INNER = 512
SSD_GROUPS = 2
SSD_STATE = 128
SSD_CONV = 4
SSD_CONV_DIM = 1024
XA_HEADS = 4
XA_HEAD_DIM = 256
N_EXPERT_GROUPS = 4
EXPERTS_PER_GROUP = 8
N_EXPERTS = 32
D_EXPERT = 256
DEPTH = 1
ALPHA = (2.0 * DEPTH) ** 0.25

LANES = 128
V7X_VMEM_LIMIT = 56 * 1024 * 1024

C_CQ = 0
C_CKV = 256
C_Z = 384
C_XBC = 896
C_KPE = 1920
C_KPER = 2048
C_DT = 2176
IN_COLS_R = 2304

TM_IN = 512
Q_BLK = 256
SSD_L = 256
TM_MID = 512
MID_SUB = 512
TM_MOE = 256
PAIRS_PER_GROUP = EXPERTS_PER_GROUP * (EXPERTS_PER_GROUP - 1) // 2
N_BUCKETS = LANES
SCATTER_CHUNK = 2048
UNSORT_CHUNK = 1024
ROW_SUB = D_MODEL // LANES


def _cparams(sem):
    return pltpu.CompilerParams(dimension_semantics=sem, vmem_limit_bytes=V7X_VMEM_LIMIT)


def _sigmoid(x):
    return 1.0 / (1.0 + jnp.exp(-x))


def _split3(x):
    hi = x.astype(BF16)
    r = x - hi.astype(F32)
    mid = r.astype(BF16)
    return hi, mid, (r - mid.astype(F32)).astype(BF16)


def _row(t):
    return pl.ds(pl.multiple_of(t * ROW_SUB, ROW_SUB), ROW_SUB)


def _lane_block(k, n):
    return pl.ds(k, n, stride=ROW_SUB)


def _layer_norm(x, g, b):
    mu = jnp.mean(x, axis=-1, keepdims=True)
    xc = x - mu
    var = jnp.mean(xc * xc, axis=-1, keepdims=True)
    return xc * lax.rsqrt(var + EPS) * g + b


def _fold_kernel(wqn_ref, wukt_ref, wuv_ref, woa_ref, gkv_row_ref, gkv_col_ref, wabs_ref, wof_ref, *, scale):
    for h in range(MLA_HEADS):
        wabs_ref[h] = scale * jnp.dot(wqn_ref[h], wukt_ref[h] * gkv_row_ref[...],
                                      precision=HIGHEST, preferred_element_type=F32)
        wof_ref[h] = jnp.dot(wuv_ref[h] * gkv_col_ref[...], woa_ref[h],
                             precision=HIGHEST, preferred_element_type=F32)


def _fold_weights(w_q_up, w_kv_up, mla_kv_norm, w_out, scale):
    wq = w_q_up.reshape(MLA_Q_RANK, MLA_HEADS, MLA_NOPE + MLA_ROPE)
    wkv = w_kv_up.reshape(MLA_KV_RANK, MLA_HEADS, MLA_NOPE + MLA_V)
    wqn = jnp.transpose(wq[:, :, :MLA_NOPE], (1, 0, 2))
    wukt = jnp.transpose(wkv[:, :, :MLA_NOPE], (1, 2, 0))
    wuv = jnp.transpose(wkv[:, :, MLA_NOPE:], (1, 0, 2))
    woa = w_out[:MLA_HEADS * MLA_V].reshape(MLA_HEADS, MLA_V, D_MODEL)
    wabs, wof = pl.pallas_call(
        functools.partial(_fold_kernel, scale=scale),
        out_shape=(jax.ShapeDtypeStruct((MLA_HEADS, MLA_Q_RANK, MLA_KV_RANK), F32),
                   jax.ShapeDtypeStruct((MLA_HEADS, MLA_KV_RANK, D_MODEL), F32)),
        name="fold",
    )(wqn, wukt, wuv, woa, mla_kv_norm.reshape(1, MLA_KV_RANK), mla_kv_norm.reshape(MLA_KV_RANK, 1))
    w_abs = jnp.transpose(wabs, (1, 0, 2)).reshape(MLA_Q_RANK, MLA_HEADS * MLA_KV_RANK).astype(BF16)
    w_pe3 = wq[:, :, MLA_NOPE:] * scale
    half = MLA_ROPE // 2
    w_per3 = jnp.concatenate([-w_pe3[:, :, half:], w_pe3[:, :, :half]], axis=-1)
    w_pe = w_pe3.reshape(MLA_Q_RANK, MLA_HEADS * MLA_ROPE).astype(BF16)
    w_per = w_per3.reshape(MLA_Q_RANK, MLA_HEADS * MLA_ROPE).astype(BF16)
    w_mix = jnp.concatenate([wof.reshape(MLA_HEADS * MLA_KV_RANK, D_MODEL), w_out[MLA_HEADS * MLA_V:]],
                            axis=0).astype(BF16)
    return w_abs, w_pe, w_per, w_mix


def _inproj_kernel(x_ref, w_ref, gq_ref, cw_ref, cb_ref, dtb_ref, cos_ref, sin_ref,
                   cq_ref, kc_ref, z_ref, xbc_ref, dt_ref, cbuf):
    j = pl.program_id(1)
    tm = x_ref.shape[0]
    @pl.when(j == 0)
    def _():
        cbuf[0:8, :] = jnp.zeros((8, SSD_CONV_DIM), F32)

    xb = x_ref[...].astype(BF16)
    seg = lambda lo, n: jnp.dot(xb, w_ref[:, lo:lo + n], preferred_element_type=F32)

    u = seg(C_XBC, SSD_CONV_DIM)
    cbuf[8:8 + tm, :] = u
    acc = cb_ref[...] + cw_ref[SSD_CONV - 1:SSD_CONV, :] * u
    for k in range(SSD_CONV - 1):
        acc = acc + cw_ref[k:k + 1, :] * cbuf[pl.ds(8 - (SSD_CONV - 1) + k, tm), :]
    xbc_ref[...] = (acc * _sigmoid(acc)).astype(BF16)
    cbuf[0:8, :] = cbuf[tm:tm + 8, :]

    z_ref[...] = seg(C_Z, SSD_INNER).astype(BF16)

    tail = seg(C_KPE, 3 * LANES)
    dtr = tail[:, 2 * LANES:] + dtb_ref[...]
    dt_ref[...] = jnp.maximum(dtr, 0.0) + jnp.log(1.0 + jnp.exp(-jnp.abs(dtr)))
    kpe = tail[:, :LANES] * cos_ref[...] + tail[:, LANES:2 * LANES] * sin_ref[...]

    lat = seg(C_CQ, MLA_Q_RANK + MLA_KV_RANK)
    c_q = lat[:, :MLA_Q_RANK]
    cq = c_q * lax.rsqrt(jnp.mean(c_q * c_q, axis=-1, keepdims=True) + EPS) * gq_ref[...]
    cq_ref[...] = cq.astype(BF16)
    c_kv = lat[:, MLA_Q_RANK:]
    ckv = c_kv * lax.rsqrt(jnp.mean(c_kv * c_kv, axis=-1, keepdims=True) + EPS)
    kc_ref[...] = jnp.concatenate([ckv, kpe], axis=1).astype(BF16)


def _inproj(x, w_in_r, gq, conv_w, conv_b, dt_bias_p, cos4, sin4):
    b, s, d = x.shape
    tm = TM_IN
    grid = (b, s // tm)
    tok = lambda w: pl.BlockSpec((None, tm, w), lambda bi, j: (bi, j, 0))
    full = lambda a: pl.BlockSpec(a.shape, lambda bi, j: (0,) * a.ndim)
    pos = pl.BlockSpec((tm, LANES), lambda bi, j: (j, 0))
    return pl.pallas_call(
        _inproj_kernel,
        grid=grid,
        in_specs=[tok(d), full(w_in_r), full(gq), full(conv_w), full(conv_b), full(dt_bias_p), pos, pos],
        out_specs=(tok(MLA_Q_RANK), tok(2 * LANES), tok(SSD_INNER), tok(SSD_CONV_DIM), tok(LANES)),
        out_shape=(jax.ShapeDtypeStruct((b, s, MLA_Q_RANK), BF16),
                   jax.ShapeDtypeStruct((b, s, 2 * LANES), BF16),
                   jax.ShapeDtypeStruct((b, s, SSD_INNER), BF16),
                   jax.ShapeDtypeStruct((b, s, SSD_CONV_DIM), BF16),
                   jax.ShapeDtypeStruct((b, s, LANES), F32)),
        scratch_shapes=[pltpu.VMEM((tm + 8, SSD_CONV_DIM), F32)],
        compiler_params=_cparams(("arbitrary", "arbitrary")),
        name="inproj",
    )(x, w_in_r, gq, conv_w, conv_b, dt_bias_p, cos4, sin4)


def _attn_kernel(cq_ref, kc_ref, wabs_ref, wpe_ref, wper_ref, cos_ref, sin_ref, o_ref,
                 q_scr, s_scr, mx_scr, ls_scr, acc_scr):
    i = pl.program_id(1)
    qb = cq_ref.shape[0]
    rows = MLA_HEADS * qb
    cq = cq_ref[...]
    qabs = jnp.dot(cq, wabs_ref[...], preferred_element_type=F32)
    cos8 = jnp.concatenate([cos_ref[...], cos_ref[...]], axis=1)
    sin8 = jnp.concatenate([sin_ref[...], sin_ref[...]], axis=1)
    qpe = (jnp.dot(cq, wpe_ref[...], preferred_element_type=F32) * cos8
           + jnp.dot(cq, wper_ref[...], preferred_element_type=F32) * sin8)
    lane = lax.broadcasted_iota(jnp.int32, (qb, LANES), 1)
    heads_per_blk = LANES // MLA_ROPE
    for h in range(MLA_HEADS):
        pe_blk = qpe[:, LANES * (h // heads_per_blk):LANES * (h // heads_per_blk + 1)]
        pe_h = jnp.where(lane // MLA_ROPE == h % heads_per_blk, pe_blk, 0.0)
        q_scr[h * qb:(h + 1) * qb, :] = jnp.concatenate(
            [qabs[:, h * MLA_KV_RANK:(h + 1) * MLA_KV_RANK], pe_h], axis=1).astype(BF16)

    def scores(j):
        kblk = kc_ref[pl.ds(pl.multiple_of(j * qb, qb), qb), :]
        return lax.dot_general(q_scr[...], kblk, (((1,), (1,)), ((), ())), preferred_element_type=F32)

    def lane_max(s):
        return jnp.maximum(s[:, :LANES], s[:, LANES:])

    mx_scr[...] = jnp.full((rows, LANES), -jnp.inf, F32)

    def pass1(j, carry):
        s = scores(j)
        s_scr[j] = s
        mx_scr[...] = jnp.maximum(mx_scr[...], lane_max(s))
        return carry

    lax.fori_loop(0, i, pass1, 0)
    r = lax.broadcasted_iota(jnp.int32, (qb, qb), 0)
    c = lax.broadcasted_iota(jnp.int32, (qb, qb), 1)
    visible = c // CHUNK <= r // CHUNK
    s = scores(i)
    s = jnp.concatenate([jnp.where(visible, s[h * qb:(h + 1) * qb], NEG_INF) for h in range(MLA_HEADS)],
                        axis=0)
    s_scr[i] = s
    m = jnp.max(jnp.maximum(mx_scr[...], lane_max(s)), axis=-1, keepdims=True)
    mx_scr[...] = jnp.broadcast_to(m, (rows, LANES))

    ls_scr[...] = jnp.zeros((rows, LANES), F32)
    acc_scr[...] = jnp.zeros((rows, MLA_KV_RANK), F32)

    def pass2(j, carry):
        sj = s_scr[j]
        mb = mx_scr[...]
        p = jnp.concatenate([jnp.exp2(sj[:, :LANES] - mb), jnp.exp2(sj[:, LANES:] - mb)], axis=1)
        ls_scr[...] = ls_scr[...] + p[:, :LANES] + p[:, LANES:]
        vblk = kc_ref[pl.ds(pl.multiple_of(j * qb, qb), qb), 0:MLA_KV_RANK]
        acc_scr[...] = acc_scr[...] + jnp.dot(p.astype(BF16), vblk, preferred_element_type=F32)
        return carry

    lax.fori_loop(0, i + 1, pass2, 0)

    o = acc_scr[...] * (1.0 / jnp.sum(ls_scr[...], axis=-1, keepdims=True))
    for h in range(MLA_HEADS):
        o_ref[:, h * MLA_KV_RANK:(h + 1) * MLA_KV_RANK] = o[h * qb:(h + 1) * qb, :].astype(BF16)


def _attention(cq, kc, w_abs, w_pe, w_per, cos4, sin4):
    b, s, _ = cq.shape
    qb = Q_BLK
    rows = MLA_HEADS * qb
    full = lambda a: pl.BlockSpec(a.shape, lambda bi, i: (0,) * a.ndim)
    pos = pl.BlockSpec((qb, LANES), lambda bi, i: (i, 0))
    return pl.pallas_call(
        _attn_kernel,
        grid=(b, s // qb),
        in_specs=[pl.BlockSpec((None, qb, MLA_Q_RANK), lambda bi, i: (bi, i, 0)),
                  pl.BlockSpec((None, s, 2 * LANES), lambda bi, i: (bi, 0, 0)),
                  full(w_abs), full(w_pe), full(w_per), pos, pos],
        out_specs=pl.BlockSpec((None, qb, MLA_HEADS * MLA_KV_RANK), lambda bi, i: (bi, i, 0)),
        out_shape=jax.ShapeDtypeStruct((b, s, MLA_HEADS * MLA_KV_RANK), BF16),
        scratch_shapes=[pltpu.VMEM((rows, 2 * LANES), BF16),
                        pltpu.VMEM((s // qb, rows, qb), F32),
                        pltpu.VMEM((rows, LANES), F32),
                        pltpu.VMEM((rows, LANES), F32),
                        pltpu.VMEM((rows, MLA_KV_RANK), F32)],
        compiler_params=_cparams(("arbitrary", "arbitrary")),
        name="attn",
    )(cq, kc, w_abs, w_pe, w_per, cos4, sin4)


def _ssd_kernel(xbc_ref, dt_ref, z_ref, alog_ref, dfull_ref, norm_ref, e_ref, o_ref, state_scr):
    c = pl.program_id(1)
    ln = xbc_ref.shape[0]
    gw = SSD_INNER // SSD_GROUPS
    hpg = SSD_HEADS // SSD_GROUPS

    @pl.when(c == 0)
    def _():
        state_scr[...] = jnp.zeros(state_scr.shape, F32)

    xs = xbc_ref[:, 0:SSD_INNER].astype(F32)
    dt = dt_ref[...]
    lane1 = lax.broadcasted_iota(jnp.int32, (1, LANES), 1)
    a = jnp.where(lane1 < SSD_HEADS, -jnp.exp(alog_ref[...]), 0.0)
    adt = dt * a
    row = lax.broadcasted_iota(jnp.int32, (ln, ln), 0)
    col = lax.broadcasted_iota(jnp.int32, (ln, ln), 1)
    causal = col <= row
    tril = causal.astype(BF16)
    e = e_ref[...]
    acs = sum(jnp.dot(tril, p, preferred_element_type=F32) for p in _split3(adt))
    acs_e = sum(jnp.dot(p, e, preferred_element_type=F32) for p in _split3(acs))
    dt_e = sum(jnp.dot(p, e, preferred_element_type=F32) for p in _split3(dt))
    acs_end = acs_e[ln - 1:ln, :]
    xdt = xs * dt_e
    x_end = (xdt * jnp.exp(acs_end - acs_e)).astype(BF16)
    eacs = jnp.exp(acs_e)
    chunk_decay = jnp.exp(acs_end)
    acs_t = acs.T
    lane_g = lax.broadcasted_iota(jnp.int32, (ln, gw), 1)

    ys = []
    for g in range(SSD_GROUPS):
        bg = xbc_ref[:, SSD_INNER + g * SSD_STATE:SSD_INNER + (g + 1) * SSD_STATE]
        cg = xbc_ref[:, SSD_INNER + SSD_GROUPS * SSD_STATE + g * SSD_STATE:
                     SSD_INNER + SSD_GROUPS * SSD_STATE + (g + 1) * SSD_STATE]
        cb = lax.dot_general(cg, bg, (((1,), (1,)), ((), ())), preferred_element_type=F32)
        prev = state_scr[g]
        y = jnp.dot(cg, prev.astype(BF16), preferred_element_type=F32) * eacs[:, g * gw:(g + 1) * gw]
        xg = xdt[:, g * gw:(g + 1) * gw]
        for hh in range(hpg):
            h = g * hpg + hh
            seg = acs[:, h:h + 1] - acs_t[h:h + 1, :]
            dec = jnp.exp(jnp.where(causal, seg, -jnp.inf))
            xm = jnp.where(lane_g // SSD_HEADDIM == hh, xg, 0.0).astype(BF16)
            y = y + jnp.dot((cb * dec).astype(BF16), xm, preferred_element_type=F32)
        st = lax.dot_general(bg, x_end[:, g * gw:(g + 1) * gw], (((0,), (0,)), ((), ())),
                             preferred_element_type=F32)
        state_scr[g] = chunk_decay[:, g * gw:(g + 1) * gw] * prev + st
        ys.append(y)

    y = jnp.concatenate(ys, axis=1) + dfull_ref[...] * xs
    zf = z_ref[...].astype(F32)
    yz = y * (zf * _sigmoid(zf))
    outs = []
    for g in range(SSD_GROUPS):
        yg = yz[:, g * gw:(g + 1) * gw]
        outs.append(yg * lax.rsqrt(jnp.mean(yg * yg, axis=-1, keepdims=True) + EPS))
    o_ref[...] = (jnp.concatenate(outs, axis=1) * norm_ref[...]).astype(BF16)


def _ssd(xbc, dt, z, alog_p, d_full, ssd_norm, expand):
    b, s, _ = xbc.shape
    ln = SSD_L
    tok = lambda w: pl.BlockSpec((None, ln, w), lambda bi, c: (bi, c, 0))
    full = lambda a: pl.BlockSpec(a.shape, lambda bi, c: (0,) * a.ndim)
    return pl.pallas_call(
        _ssd_kernel,
        grid=(b, s // ln),
        in_specs=[tok(SSD_CONV_DIM), tok(LANES), tok(SSD_INNER), full(alog_p), full(d_full), full(ssd_norm),
                  full(expand)],
        out_specs=tok(SSD_INNER),
        out_shape=jax.ShapeDtypeStruct((b, s, SSD_INNER), BF16),
        scratch_shapes=[pltpu.VMEM((SSD_GROUPS, SSD_STATE, SSD_INNER // SSD_GROUPS), F32)],
        compiler_params=_cparams(("arbitrary", "arbitrary")),
        name="ssd",
    )(xbc, dt, z, alog_p, d_full, ssd_norm, expand)


def _memkv_kernel(mem_ref, wk_ref, wv_ref, k_ref, v_ref):
    m = mem_ref[...].astype(BF16)
    k_ref[...] = jnp.dot(m, wk_ref[...], preferred_element_type=F32).astype(BF16)
    v_ref[...] = jnp.dot(m, wv_ref[...], preferred_element_type=F32).astype(BF16)


def _memkv(mem, wk, wv):
    b, m, d = mem.shape
    blk = pl.BlockSpec((None, m, d), lambda bi: (bi, 0, 0))
    full = lambda a: pl.BlockSpec(a.shape, lambda bi: (0,) * a.ndim)
    return pl.pallas_call(
        _memkv_kernel,
        grid=(b,),
        in_specs=[blk, full(wk), full(wv)],
        out_specs=(blk, blk),
        out_shape=(jax.ShapeDtypeStruct((b, m, d), BF16), jax.ShapeDtypeStruct((b, m, d), BF16)),
        compiler_params=_cparams(("arbitrary",)),
        name="memkv",
    )(mem, wk, wv)


def _split_bf16(w):
    hi = w.astype(BF16)
    return hi, (w - hi.astype(F32)).astype(BF16)


def _router_logits(h, wrh_ref, wrl_ref, br_ref):
    h_hi, h_lo = _split_bf16(h)
    return (jnp.dot(h_hi, wrh_ref[...], preferred_element_type=F32)
            + jnp.dot(h_hi, wrl_ref[...], preferred_element_type=F32)
            + jnp.dot(h_lo, wrh_ref[...], preferred_element_type=F32)) + br_ref[...]


def _is_group_lane(lane):
    return (lane >= N_EXPERTS) & (lane < N_EXPERTS + N_EXPERT_GROUPS)


def _mid_kernel(x_ref, ol_ref, ss_ref, wmix_ref, g1_ref, b1_ref, wq_ref, km_ref, vm_ref, wo_ref,
                g2_ref, b2_ref, wrh_ref, wrl_ref, br_ref, row_ref, meta_ref, cnt_ref, run_scr):
    step = pl.program_id(0) * pl.num_programs(1) + pl.program_id(1)
    tm = x_ref.shape[0]
    n_lat = MLA_HEADS * MLA_KV_RANK

    @pl.when(step == 0)
    def _():
        run_scr[...] = jnp.zeros(run_scr.shape, F32)

    ns = MID_SUB
    lane = lax.broadcasted_iota(jnp.int32, (ns, LANES), 1)
    is_g = _is_group_lane(lane)
    r = lax.broadcasted_iota(jnp.int32, (ns, ns), 0)
    c = lax.broadcasted_iota(jnp.int32, (ns, ns), 1)
    strictly_before = (c < r).astype(BF16)
    run = run_scr[...]
    for sub in range(tm // ns):
        rs = slice(sub * ns, (sub + 1) * ns)
        mix = (jnp.dot(ol_ref[rs, :], wmix_ref[0:n_lat, :], preferred_element_type=F32)
               + jnp.dot(ss_ref[rs, :], wmix_ref[n_lat:, :], preferred_element_type=F32))
        h1 = _layer_norm(ALPHA * x_ref[rs, :] + mix, g1_ref[...], b1_ref[...])

        q = jnp.dot(h1.astype(BF16), wq_ref[...], preferred_element_type=F32).astype(BF16)
        outs = []
        for h in range(XA_HEADS):
            sl = slice(h * XA_HEAD_DIM, (h + 1) * XA_HEAD_DIM)
            s = lax.dot_general(q[:, sl], km_ref[:, sl], (((1,), (1,)), ((), ())), preferred_element_type=F32)
            p = jnp.exp(s - jnp.max(s, axis=-1, keepdims=True))
            o = jnp.dot(p.astype(BF16), vm_ref[:, sl], preferred_element_type=F32)
            outs.append((o / jnp.sum(p, axis=-1, keepdims=True)).astype(BF16))
        xa = jnp.dot(jnp.concatenate(outs, axis=1), wo_ref[...], preferred_element_type=F32)
        h2 = _layer_norm(ALPHA * h1 + xa, g2_ref[...], b2_ref[...])

        logits = _router_logits(h2, wrh_ref, wrl_ref, br_ref)
        gmax = jnp.max(jnp.where(is_g, logits, -jnp.inf), axis=-1, keepdims=True)
        g_idx = jnp.min(jnp.where(is_g & (logits == gmax), lane - N_EXPERTS, LANES), axis=-1, keepdims=True)
        in_grp = (lane < N_EXPERTS) & (lane // EXPERTS_PER_GROUP == g_idx)
        t1 = jnp.max(jnp.where(in_grp, logits, -jnp.inf), axis=-1, keepdims=True)
        i1 = jnp.min(jnp.where(in_grp & (logits == t1), lane, LANES), axis=-1, keepdims=True)
        rest = in_grp & (lane != i1)
        t2 = jnp.max(jnp.where(rest, logits, -jnp.inf), axis=-1, keepdims=True)
        i2 = jnp.min(jnp.where(rest & (logits == t2), lane, LANES), axis=-1, keepdims=True)
        ea = jnp.minimum(i1, i2) - g_idx * EXPERTS_PER_GROUP
        eb = jnp.maximum(i1, i2) - g_idx * EXPERTS_PER_GROUP
        pair = ea * EXPERTS_PER_GROUP - (ea * (ea + 1)) // 2 + eb - ea - 1
        bucket = g_idx * PAIRS_PER_GROUP + pair
        onehot = (lane == bucket).astype(BF16)
        before = jnp.dot(strictly_before, onehot, preferred_element_type=F32) + run
        rank = jnp.sum(jnp.where(lane == bucket, before, 0.0), axis=-1, keepdims=True)
        run = run + jnp.sum(onehot.astype(F32), axis=0, keepdims=True)
        meta_ref[rs, :] = jnp.where(lane == 0, bucket.astype(F32), jnp.where(lane == 1, rank, 0.0))

        for k in range(ROW_SUB):
            row_ref[pl.ds(sub * ns * ROW_SUB + k, ns, stride=ROW_SUB), :] = h2[:, k * LANES:(k + 1) * LANES]
    run_scr[...] = run
    cnt_ref[...] = run


def _mid(x, o_lat, ssd_out, w_mix, g1, b1, wq, kmem, vmem, wo, g2, b2, wrh, wrl, br):
    b, s, d = x.shape
    tm = TM_MID
    tok = lambda w: pl.BlockSpec((None, tm, w), lambda bi, j: (bi, j, 0))
    full = lambda a: pl.BlockSpec(a.shape, lambda bi, j: (0,) * a.ndim)
    mem = pl.BlockSpec((None,) + kmem.shape[1:], lambda bi, j: (bi, 0, 0))
    tok_rows = lambda n: pl.BlockSpec((None, n * ROW_SUB, LANES), lambda bi, j: (bi, j, 0))
    return pl.pallas_call(
        _mid_kernel,
        grid=(b, s // tm),
        in_specs=[tok(d), tok(o_lat.shape[-1]), tok(SSD_INNER), full(w_mix), full(g1), full(b1), full(wq),
                  mem, mem, full(wo), full(g2), full(b2), full(wrh), full(wrl), full(br)],
        out_specs=(tok_rows(tm), tok(LANES), pl.BlockSpec((1, LANES), lambda bi, j: (0, 0))),
        out_shape=(jax.ShapeDtypeStruct((b, s * ROW_SUB, LANES), F32),
                   jax.ShapeDtypeStruct((b, s, LANES), F32),
                   jax.ShapeDtypeStruct((1, LANES), F32)),
        scratch_shapes=[pltpu.VMEM((1, LANES), F32)],
        compiler_params=_cparams(("arbitrary", "arbitrary")),
        name="mid",
    )(x, o_lat, ssd_out, w_mix, g1, b1, wq, kmem, vmem, wo, g2, b2, wrh, wrl, br)


def _scatter_kernel(dest_ref, ztile_ref, rows_ref, out_hbm, zero_scr, sem, zsem):
    c = pl.program_id(0)
    n = dest_ref.shape[1]
    tile_rows = zero_scr.shape[0]

    @pl.when(c == 0)
    def _():
        zero_scr[...] = jnp.zeros(zero_scr.shape, F32)

        def zero_copy(k):
            start = pl.multiple_of(jnp.maximum(ztile_ref[0, k], 0) * tile_rows, tile_rows)
            return pltpu.make_async_copy(zero_scr, out_hbm.at[pl.ds(start, tile_rows)], zsem)

        def issue_zero(k, carry):
            @pl.when(ztile_ref[0, k] >= 0)
            def _():
                zero_copy(k).start()
            return carry

        def wait_zero(k, carry):
            @pl.when(ztile_ref[0, k] >= 0)
            def _():
                zero_copy(k).wait()
            return carry

        lax.fori_loop(0, ztile_ref.shape[1], issue_zero, 0)
        lax.fori_loop(0, ztile_ref.shape[1], wait_zero, 0)

    def issue(u, carry):
        for p in range(2):
            t = 2 * u + p
            pltpu.make_async_copy(rows_ref.at[_row(t)], out_hbm.at[_row(dest_ref[0, t])], sem).start(priority=p)
        return carry

    lax.fori_loop(0, n // 2, issue, 0)
    pltpu.make_async_copy(rows_ref, out_hbm.at[pl.ds(0, n * ROW_SUB)], sem).wait()


def _scatter_rows(rows, dest3, ztile, n_out, tile):
    t = rows.shape[0] // ROW_SUB
    n = dest3.shape[-1]
    return pl.pallas_call(
        _scatter_kernel,
        grid=(t // n,),
        in_specs=[pl.BlockSpec((None, 1, n), lambda c: (c, 0, 0), memory_space=pltpu.SMEM),
                  pl.BlockSpec(ztile.shape, lambda c: (0, 0), memory_space=pltpu.SMEM),
                  pl.BlockSpec((n * ROW_SUB, LANES), lambda c: (c, 0))],
        out_specs=pl.BlockSpec(memory_space=pl.ANY),
        out_shape=jax.ShapeDtypeStruct((n_out * ROW_SUB, LANES), rows.dtype),
        scratch_shapes=[pltpu.VMEM((tile * ROW_SUB, LANES), F32), pltpu.SemaphoreType.DMA(()),
                        pltpu.SemaphoreType.DMA(())],
        compiler_params=_cparams(("arbitrary",)),
        name="scatter",
    )(dest3, ztile, rows)


def _unsort_kernel(dcur_ref, dnext_ref, rows_hbm, o_ref, buf, sem):
    c = pl.program_id(0)
    n = o_ref.shape[0]
    slot = c % 2

    def issue(dref, sl):
        def body(u, carry):
            for p in range(2):
                t = 2 * u + p
                pltpu.make_async_copy(rows_hbm.at[_row(dref[0, t])], buf.at[sl, _row(t)],
                                      sem.at[sl]).start(priority=p)
            return carry

        lax.fori_loop(0, n // 2, body, 0)

    @pl.when(c == 0)
    def _():
        issue(dcur_ref, 0)

    @pl.when(c + 1 < pl.num_programs(0))
    def _():
        issue(dnext_ref, 1 - slot)

    pltpu.make_async_copy(rows_hbm.at[pl.ds(0, n * ROW_SUB)], buf.at[slot], sem.at[slot]).wait()
    for k in range(ROW_SUB):
        o_ref[:, k * LANES:(k + 1) * LANES] = buf[slot, _lane_block(k, n), :]


def _unsort_rows(rows, dest3, n_tok):
    n = dest3.shape[-1]
    nc = n_tok // n
    return pl.pallas_call(
        _unsort_kernel,
        grid=(nc,),
        in_specs=[pl.BlockSpec((None, 1, n), lambda c: (c, 0, 0), memory_space=pltpu.SMEM),
                  pl.BlockSpec((None, 1, n), lambda c: (jnp.minimum(c + 1, nc - 1), 0, 0),
                               memory_space=pltpu.SMEM),
                  pl.BlockSpec(memory_space=pl.ANY)],
        out_specs=pl.BlockSpec((n, D_MODEL), lambda c: (c, 0)),
        out_shape=jax.ShapeDtypeStruct((n_tok, D_MODEL), rows.dtype),
        scratch_shapes=[pltpu.VMEM((2, n * ROW_SUB, LANES), F32), pltpu.SemaphoreType.DMA((2,))],
        compiler_params=_cparams(("arbitrary",)),
        name="unsort",
    )(dest3, dest3, rows)


def _moe_kernel(e1_ref, e2_ref, nused_ref, rows_ref, wr_ref, br_ref, wg1_ref, wu1_ref, wd1_ref,
                wg2_ref, wu2_ref, wd2_ref, g3_ref, b3_ref, o_ref):
    i = pl.program_id(0)
    tm = rows_ref.shape[0] // ROW_SUB

    @pl.when(i < nused_ref[0])
    def _():
        e1 = e1_ref[i]
        e2 = e2_ref[i]
        g = e1 // EXPERTS_PER_GROUP
        x = jnp.concatenate([rows_ref[_lane_block(k, tm), :] for k in range(ROW_SUB)], axis=1)
        xb = x.astype(BF16)

        logits = jnp.dot(xb, wr_ref[...], preferred_element_type=F32) + br_ref[...]
        lane = lax.broadcasted_iota(jnp.int32, (tm, LANES), 1)
        is_g = _is_group_lane(lane)
        pick = lambda idx: jnp.sum(jnp.where(lane == idx, logits, 0.0), axis=-1, keepdims=True)
        gmax = jnp.max(jnp.where(is_g, logits, -jnp.inf), axis=-1, keepdims=True)
        g_gate = jnp.exp(pick(N_EXPERTS + g) - gmax) / jnp.sum(jnp.where(is_g, jnp.exp(logits - gmax), 0.0),
                                                                axis=-1, keepdims=True)
        t1 = pick(e1)
        t2 = pick(e2)
        c1 = g_gate / (1.0 + jnp.exp(t2 - t1))
        c2 = g_gate / (1.0 + jnp.exp(t1 - t2))

        def expert(wg_ref, wu_ref, wd_ref, ce):
            hg = jnp.dot(xb, wg_ref[...], preferred_element_type=F32)
            hu = jnp.dot(xb, wu_ref[...], preferred_element_type=F32)
            hd = hg * _sigmoid(hg) * hu * ce
            return jnp.dot(hd.astype(BF16), wd_ref[...], preferred_element_type=F32)

        acc = expert(wg1_ref, wu1_ref, wd1_ref, c1) + expert(wg2_ref, wu2_ref, wd2_ref, c2)
        y = _layer_norm(ALPHA * x + acc, g3_ref[...], b3_ref[...])
        for k in range(ROW_SUB):
            o_ref[_lane_block(k, tm), :] = y[:, k * LANES:(k + 1) * LANES]

    @pl.when(i >= nused_ref[0])
    def _():
        o_ref[...] = jnp.zeros(o_ref.shape, F32)


def _moe(rows_sorted, tile_e1, tile_e2, n_used, wr, br, wg, wu, wd, g3, b3):
    n_rows = rows_sorted.shape[0] // ROW_SUB
    tm = TM_MOE
    nt = n_rows // tm

    def tile_map(i, e1, e2, nused):
        return (jnp.maximum(jnp.minimum(i, nused[0] - 1), 0), 0)

    first = lambda i, e1, e2, nused: (e1[i], 0, 0)
    second = lambda i, e1, e2, nused: (e2[i], 0, 0)
    full = lambda a: pl.BlockSpec(a.shape, lambda i, e1, e2, nused: (0,) * a.ndim)
    w_in = lambda m: pl.BlockSpec((None, D_MODEL, D_EXPERT), m)
    w_out = lambda m: pl.BlockSpec((None, D_EXPERT, D_MODEL), m)
    grid_spec = pltpu.PrefetchScalarGridSpec(
        num_scalar_prefetch=3,
        grid=(nt,),
        in_specs=[pl.BlockSpec((tm * ROW_SUB, LANES), tile_map), full(wr), full(br),
                  w_in(first), w_in(first), w_out(first), w_in(second), w_in(second), w_out(second),
                  full(g3), full(b3)],
        out_specs=pl.BlockSpec((tm * ROW_SUB, LANES), lambda i, e1, e2, nused: (i, 0)),
    )
    return pl.pallas_call(
        _moe_kernel,
        grid_spec=grid_spec,
        out_shape=jax.ShapeDtypeStruct((n_rows * ROW_SUB, LANES), F32),
        compiler_params=_cparams(("arbitrary",)),
        name="moe",
    )(tile_e1, tile_e2, n_used, rows_sorted, wr, br, wg, wu, wd, wg, wu, wd, g3, b3)


def _rope_tables(seq):
    pos = jnp.arange(seq, dtype=F32)
    inv_freq = ROPE_THETA ** (-jnp.arange(0, MLA_ROPE, 2, dtype=F32) / MLA_ROPE)
    ang = pos[:, None] * inv_freq[None, :]
    cos = jnp.tile(jnp.cos(ang), (1, LANES // (MLA_ROPE // 2)))
    sin = jnp.tile(jnp.sin(ang), (1, LANES // (MLA_ROPE // 2)))
    return cos, sin


def _pad_lanes(v, n=LANES):
    v = v.reshape(1, -1)
    return jnp.pad(v, ((0, 0), (0, n - v.shape[1])))


def kernel(x, mem, w_in, mla_q_norm, w_q_up, mla_kv_norm, w_kv_up, ssd_conv_w, ssd_conv_b, ssd_dt_bias, ssd_a_log, ssd_d, ssd_norm, w_out, ln1_g, ln1_b, xa_wq, xa_wk, xa_wv, xa_wo, ln2_g, ln2_b, router_group_w, router_group_b, router_expert_w, router_expert_b, expert_w_gate, expert_w_up, expert_w_down, ln3_g, ln3_b):
    b, s, d = x.shape
    t = b * s
    l = 0
    row = lambda v: v[l].reshape(1, -1)

    wi = w_in[l]
    o1 = MLA_Q_RANK
    o2 = o1 + MLA_KV_RANK
    o3 = o2 + MLA_ROPE
    o4 = o3 + SSD_INNER
    o5 = o4 + SSD_CONV_DIM
    w_kr = wi[:, o2:o3]
    half = MLA_ROPE // 2
    w_kr_rot = jnp.concatenate([-w_kr[:, half:], w_kr[:, :half]], axis=1)
    reps = LANES // MLA_ROPE
    w_in_r = jnp.concatenate(
        [wi[:, :o2], wi[:, o3:o4], wi[:, o4:o5], jnp.tile(w_kr, (1, reps)), jnp.tile(w_kr_rot, (1, reps)),
         wi[:, o5:], jnp.zeros((d, LANES - SSD_HEADS), F32)], axis=1).astype(BF16)
    cos4, sin4 = _rope_tables(s)
    attn_scale = (MLA_NOPE + MLA_ROPE) ** -0.5 * math.log2(math.e)
    w_abs, w_pe, w_per, w_mix = _fold_weights(w_q_up[l], w_kv_up[l], mla_kv_norm[l], w_out[l], attn_scale)
    expand = (jnp.arange(LANES)[:, None] == (jnp.arange(SSD_INNER)[None, :] // SSD_HEADDIM)).astype(BF16)
    d_full = jnp.repeat(ssd_d[l], SSD_HEADDIM).reshape(1, SSD_INNER)
    wrh, wrl = _split_bf16(jnp.concatenate(
        [router_expert_w[l], router_group_w[l], jnp.zeros((d, LANES - N_EXPERTS - N_EXPERT_GROUPS), F32)], axis=1))
    br = _pad_lanes(jnp.concatenate([router_expert_b[l], router_group_b[l]]))

    cq, kc, z, xbc, dt = _inproj(x, w_in_r, row(mla_q_norm), ssd_conv_w[l], row(ssd_conv_b),
                                 _pad_lanes(ssd_dt_bias[l]), cos4, sin4)
    o_lat = _attention(cq, kc, w_abs, w_pe, w_per, cos4, sin4)
    ssd_out = _ssd(xbc, dt, z, _pad_lanes(ssd_a_log[l]), d_full, row(ssd_norm), expand)

    kmem, vmem = _memkv(mem, xa_wk[l].astype(BF16), xa_wv[l].astype(BF16))
    wq = (xa_wq[l] * (XA_HEAD_DIM ** -0.5)).astype(BF16)
    rows, meta, counts = _mid(x, o_lat, ssd_out, w_mix, row(ln1_g), row(ln1_b), wq, kmem, vmem,
                              xa_wo[l].astype(BF16), row(ln2_g), row(ln2_b), wrh, wrl, br)

    tm = TM_MOE
    nt = t // tm + N_EXPERT_GROUPS * PAIRS_PER_GROUP - 1
    meta = meta.reshape(t, LANES)
    bucket = meta[:, 0].astype(jnp.int32)
    rank = meta[:, 1].astype(jnp.int32)
    cnt = counts[0].astype(jnp.int32)
    ntile = (cnt + tm - 1) // tm
    tile_end = jnp.cumsum(ntile)
    tile_start = tile_end - ntile
    first_row = jnp.sum(jnp.where(bucket[:, None] == jnp.arange(N_BUCKETS)[None, :], tile_start[None, :] * tm, 0),
                        axis=1)
    dest = first_row + rank
    n_used = tile_end[-1:].astype(jnp.int32)
    tile_bkt = jnp.minimum(jnp.sum(jnp.arange(nt)[:, None] >= tile_end[None, :], axis=1), N_BUCKETS - 1)
    pa, pb = zip(*[(a, c) for a in range(EXPERTS_PER_GROUP) for c in range(a + 1, EXPERTS_PER_GROUP)])
    bkt_g = jnp.minimum(tile_bkt // PAIRS_PER_GROUP, N_EXPERT_GROUPS - 1)
    tile_e1 = (bkt_g * EXPERTS_PER_GROUP + jnp.array(pa)[tile_bkt % PAIRS_PER_GROUP]).astype(jnp.int32)
    tile_e2 = (bkt_g * EXPERTS_PER_GROUP + jnp.array(pb)[tile_bkt % PAIRS_PER_GROUP]).astype(jnp.int32)
    spare = n_used[0] + jnp.arange(nt - t // tm)
    ztile = jnp.concatenate([jnp.where(ntile > 0, tile_end - 1, -1),
                             jnp.where(spare < nt, spare, -1)]).astype(jnp.int32).reshape(1, -1)
    sc_chunk = min(SCATTER_CHUNK, t)
    un_chunk = min(UNSORT_CHUNK, t)

    rows_sorted = _scatter_rows(rows.reshape(t * ROW_SUB, LANES), dest.reshape(t // sc_chunk, 1, sc_chunk),
                                ztile, nt * tm, tm)
    out_sorted = _moe(rows_sorted, tile_e1, tile_e2, n_used, wrh, br, expert_w_gate[l].astype(BF16),
                      expert_w_up[l].astype(BF16), expert_w_down[l].astype(BF16), row(ln3_g), row(ln3_b))
    out = _unsort_rows(out_sorted, dest.reshape(t // un_chunk, 1, un_chunk), t)
    return out.reshape(b, s, d)
```

```python
import functools
import math

import jax
import jax.numpy as jnp
from jax import lax
from jax.experimental import pallas as pl
from jax.experimental.pallas import tpu as pltpu

F32 = jnp.float32
BF16 = jnp.bfloat16
HIGHEST = lax.Precision.HIGHEST

EPS = 1e-5
NEG_INF = -1e30
CHUNK = 64
ROPE_THETA = 10000.0

D_MODEL = 1024
MLA_HEADS = 8
MLA_NOPE = 64
MLA_ROPE = 32
MLA_V = 64
MLA_Q_RANK = 256
MLA_KV_RANK = 128
SSD_HEADS = 8
SSD_HEADDIM = 64
SSD_INNER = 512
SSD_GROUPS = 2
SSD_STATE = 128
SSD_CONV = 4
SSD_CONV_DIM = 1024
XA_HEADS = 4
XA_HEAD_DIM = 256
N_EXPERT_GROUPS = 4
EXPERTS_PER_GROUP = 8
N_EXPERTS = 32
D_EXPERT = 256
DEPTH = 1
ALPHA = (2.0 * DEPTH) ** 0.25

LANES = 128
V7X_VMEM_LIMIT = 56 * 1024 * 1024

C_CQ = 0
C_CKV = 256
C_Z = 384
C_XBC = 896
C_KPE = 1920
C_KPER = 2048
C_DT = 2176
IN_COLS_R = 2304

TM_IN = 1024
Q_BLK = 256
SSD_L = 256
TM_MID = 512
MID_SUB = 512
TM_MOE = 256
PAIRS_PER_GROUP = EXPERTS_PER_GROUP * (EXPERTS_PER_GROUP - 1) // 2
N_BUCKETS = LANES
SCATTER_CHUNK = 4096
UNSORT_CHUNK = 2048
ROW_SUB = D_MODEL // LANES


def _cparams(sem):
    return pltpu.CompilerParams(dimension_semantics=sem, vmem_limit_bytes=V7X_VMEM_LIMIT)


def _sigmoid(x):
    return 1.0 / (1.0 + jnp.exp(-x))


def _split3(x):
    hi = x.astype(BF16)
    r = x - hi.astype(F32)
    mid = r.astype(BF16)
    return hi, mid, (r - mid.astype(F32)).astype(BF16)


def _row(t):
    return pl.ds(pl.multiple_of(t * ROW_SUB, ROW_SUB), ROW_SUB)


def _lane_block(k, n):
    return pl.ds(k, n, stride=ROW_SUB)


def _layer_norm(x, g, b):
    mu = jnp.mean(x, axis=-1, keepdims=True)
    xc = x - mu
    var = jnp.mean(xc * xc, axis=-1, keepdims=True)
    return xc * lax.rsqrt(var + EPS) * g + b


def _fold_kernel(wqn_ref, wukt_ref, wuv_ref, woa_ref, gkv_row_ref, gkv_col_ref, wabs_ref, wof_ref, *, scale):
    for h in range(MLA_HEADS):
        wabs_ref[h] = scale * jnp.dot(wqn_ref[h], wukt_ref[h] * gkv_row_ref[...],
                                      precision=HIGHEST, preferred_element_type=F32)
        wof_ref[h] = jnp.dot(wuv_ref[h] * gkv_col_ref[...], woa_ref[h],
                             precision=HIGHEST, preferred_element_type=F32)


def _fold_weights(w_q_up, w_kv_up, mla_kv_norm, w_out, scale):
    wq = w_q_up.reshape(MLA_Q_RANK, MLA_HEADS, MLA_NOPE + MLA_ROPE)
    wkv = w_kv_up.reshape(MLA_KV_RANK, MLA_HEADS, MLA_NOPE + MLA_V)
    wqn = jnp.transpose(wq[:, :, :MLA_NOPE], (1, 0, 2))
    wukt = jnp.transpose(wkv[:, :, :MLA_NOPE], (1, 2, 0))
    wuv = jnp.transpose(wkv[:, :, MLA_NOPE:], (1, 0, 2))
    woa = w_out[:MLA_HEADS * MLA_V].reshape(MLA_HEADS, MLA_V, D_MODEL)
    wabs, wof = pl.pallas_call(
        functools.partial(_fold_kernel, scale=scale),
        out_shape=(jax.ShapeDtypeStruct((MLA_HEADS, MLA_Q_RANK, MLA_KV_RANK), F32),
                   jax.ShapeDtypeStruct((MLA_HEADS, MLA_KV_RANK, D_MODEL), F32)),
        name="fold",
    )(wqn, wukt, wuv, woa, mla_kv_norm.reshape(1, MLA_KV_RANK), mla_kv_norm.reshape(MLA_KV_RANK, 1))
    w_abs = jnp.transpose(wabs, (1, 0, 2)).reshape(MLA_Q_RANK, MLA_HEADS * MLA_KV_RANK).astype(BF16)
    w_pe3 = wq[:, :, MLA_NOPE:] * scale
    half = MLA_ROPE // 2
    w_per3 = jnp.concatenate([-w_pe3[:, :, half:], w_pe3[:, :, :half]], axis=-1)
    w_pe = w_pe3.reshape(MLA_Q_RANK, MLA_HEADS * MLA_ROPE).astype(BF16)
    w_per = w_per3.reshape(MLA_Q_RANK, MLA_HEADS * MLA_ROPE).astype(BF16)
    w_mix = jnp.concatenate([wof.reshape(MLA_HEADS * MLA_KV_RANK, D_MODEL), w_out[MLA_HEADS * MLA_V:]],
                            axis=0).astype(BF16)
    return w_abs, w_pe, w_per, w_mix


def _inproj_kernel(x_ref, w_ref, gq_ref, cw_ref, cb_ref, dtb_ref, cos_ref, sin_ref,
                   cq_ref, kc_ref, z_ref, xbc_ref, dt_ref, cbuf):
    j = pl.program_id(1)
    tm = x_ref.shape[0]
    @pl.when(j == 0)
    def _():
        cbuf[0:8, :] = jnp.zeros((8, SSD_CONV_DIM), F32)

    xb = x_ref[...].astype(BF16)
    seg = lambda lo, n: jnp.dot(xb, w_ref[:, lo:lo + n], preferred_element_type=F32)

    u = seg(C_XBC, SSD_CONV_DIM)
    cbuf[8:8 + tm, :] = u
    acc = cb_ref[...] + cw_ref[SSD_CONV - 1:SSD_CONV, :] * u
    for k in range(SSD_CONV - 1):
        acc = acc + cw_ref[k:k + 1, :] * cbuf[pl.ds(8 - (SSD_CONV - 1) + k, tm), :]
    xbc_ref[...] = (acc * _sigmoid(acc)).astype(BF16)
    cbuf[0:8, :] = cbuf[tm:tm + 8, :]

    z_ref[...] = seg(C_Z, SSD_INNER).astype(BF16)

    tail = seg(C_KPE, 3 * LANES)
    dtr = tail[:, 2 * LANES:] + dtb_ref[...]
    dt_ref[...] = jnp.maximum(dtr, 0.0) + jnp.log(1.0 + jnp.exp(-jnp.abs(dtr)))
    kpe = tail[:, :LANES] * cos_ref[...] + tail[:, LANES:2 * LANES] * sin_ref[...]

    lat = seg(C_CQ, MLA_Q_RANK + MLA_KV_RANK)
    c_q = lat[:, :MLA_Q_RANK]
    cq = c_q * lax.rsqrt(jnp.mean(c_q * c_q, axis=-1, keepdims=True) + EPS) * gq_ref[...]
    cq_ref[...] = cq.astype(BF16)
    c_kv = lat[:, MLA_Q_RANK:]
    ckv = c_kv * lax.rsqrt(jnp.mean(c_kv * c_kv, axis=-1, keepdims=True) + EPS)
    kc_ref[...] = jnp.concatenate([ckv, kpe], axis=1).astype(BF16)


def _inproj(x, w_in_r, gq, conv_w, conv_b, dt_bias_p, cos4, sin4):
    b, s, d = x.shape
    tm = TM_IN
    grid = (b, s // tm)
    tok = lambda w: pl.BlockSpec((None, tm, w), lambda bi, j: (bi, j, 0))
    full = lambda a: pl.BlockSpec(a.shape, lambda bi, j: (0,) * a.ndim)
    pos = pl.BlockSpec((tm, LANES), lambda bi, j: (j, 0))
    return pl.pallas_call(
        _inproj_kernel,
        grid=grid,
        in_specs=[tok(d), full(w_in_r), full(gq), full(conv_w), full(conv_b), full(dt_bias_p), pos, pos],
        out_specs=(tok(MLA_Q_RANK), tok(2 * LANES), tok(SSD_INNER), tok(SSD_CONV_DIM), tok(LANES)),
        out_shape=(jax.ShapeDtypeStruct((b, s, MLA_Q_RANK), BF16),
                   jax.ShapeDtypeStruct((b, s, 2 * LANES), BF16),
                   jax.ShapeDtypeStruct((b, s, SSD_INNER), BF16),
                   jax.ShapeDtypeStruct((b, s, SSD_CONV_DIM), BF16),
                   jax.ShapeDtypeStruct((b, s, LANES), F32)),
        scratch_shapes=[pltpu.VMEM((tm + 8, SSD_CONV_DIM), F32)],
        compiler_params=_cparams(("arbitrary", "arbitrary")),
        name="inproj",
    )(x, w_in_r, gq, conv_w, conv_b, dt_bias_p, cos4, sin4)


def _attn_kernel(cq_ref, kc_ref, wabs_ref, wpe_ref, wper_ref, cos_ref, sin_ref, o_ref,
                 q_scr, s_scr, mx_scr, ls_scr, acc_scr):
    i = pl.program_id(1)
    qb = cq_ref.shape[0]
    rows = MLA_HEADS * qb
    cq = cq_ref[...]
    qabs = jnp.dot(cq, wabs_ref[...], preferred_element_type=F32)
    cos8 = jnp.concatenate([cos_ref[...], cos_ref[...]], axis=1)
    sin8 = jnp.concatenate([sin_ref[...], sin_ref[...]], axis=1)
    qpe = (jnp.dot(cq, wpe_ref[...], preferred_element_type=F32) * cos8
           + jnp.dot(cq, wper_ref[...], preferred_element_type=F32) * sin8)
    lane = lax.broadcasted_iota(jnp.int32, (qb, LANES), 1)
    heads_per_blk = LANES // MLA_ROPE
    for h in range(MLA_HEADS):
        pe_blk = qpe[:, LANES * (h // heads_per_blk):LANES * (h // heads_per_blk + 1)]
        pe_h = jnp.where(lane // MLA_ROPE == h % heads_per_blk, pe_blk, 0.0)
        q_scr[h * qb:(h + 1) * qb, :] = jnp.concatenate(
            [qabs[:, h * MLA_KV_RANK:(h + 1) * MLA_KV_RANK], pe_h], axis=1).astype(BF16)

    def scores(j):
        kblk = kc_ref[pl.ds(pl.multiple_of(j * qb, qb), qb), :]
        return lax.dot_general(q_scr[...], kblk, (((1,), (1,)), ((), ())), preferred_element_type=F32)

    def lane_max(s):
        return jnp.maximum(s[:, :LANES], s[:, LANES:])

    mx_scr[...] = jnp.full((rows, LANES), -jnp.inf, F32)

    def pass1(j, carry):
        s = scores(j)
        s_scr[j] = s
        mx_scr[...] = jnp.maximum(mx_scr[...], lane_max(s))
        return carry

    lax.fori_loop(0, i, pass1, 0)
    r = lax.broadcasted_iota(jnp.int32, (qb, qb), 0)
    c = lax.broadcasted_iota(jnp.int32, (qb, qb), 1)
    visible = c // CHUNK <= r // CHUNK
    s = scores(i)
    s = jnp.concatenate([jnp.where(visible, s[h * qb:(h + 1) * qb], NEG_INF) for h in range(MLA_HEADS)],
                        axis=0)
    s_scr[i] = s
    m = jnp.max(jnp.maximum(mx_scr[...], lane_max(s)), axis=-1, keepdims=True)
    mx_scr[...] = jnp.broadcast_to(m, (rows, LANES))

    ls_scr[...] = jnp.zeros((rows, LANES), F32)
    acc_scr[...] = jnp.zeros((rows, MLA_KV_RANK), F32)

    def pass2(j, carry):
        sj = s_scr[j]
        mb = mx_scr[...]
        p = jnp.concatenate([jnp.exp2(sj[:, :LANES] - mb), jnp.exp2(sj[:, LANES:] - mb)], axis=1)
        ls_scr[...] = ls_scr[...] + p[:, :LANES] + p[:, LANES:]
        vblk = kc_ref[pl.ds(pl.multiple_of(j * qb, qb), qb), 0:MLA_KV_RANK]
        acc_scr[...] = acc_scr[...] + jnp.dot(p.astype(BF16), vblk, preferred_element_type=F32)
        return carry

    lax.fori_loop(0, i + 1, pass2, 0)

    o = acc_scr[...] * (1.0 / jnp.sum(ls_scr[...], axis=-1, keepdims=True))
    for h in range(MLA_HEADS):
        o_ref[:, h * MLA_KV_RANK:(h + 1) * MLA_KV_RANK] = o[h * qb:(h + 1) * qb, :].astype(BF16)


def _attention(cq, kc, w_abs, w_pe, w_per, cos4, sin4):
    b, s, _ = cq.shape
    qb = Q_BLK
    rows = MLA_HEADS * qb
    full = lambda a: pl.BlockSpec(a.shape, lambda bi, i: (0,) * a.ndim)
    pos = pl.BlockSpec((qb, LANES), lambda bi, i: (i, 0))
    return pl.pallas_call(
        _attn_kernel,
        grid=(b, s // qb),
        in_specs=[pl.BlockSpec((None, qb, MLA_Q_RANK), lambda bi, i: (bi, i, 0)),
                  pl.BlockSpec((None, s, 2 * LANES), lambda bi, i: (bi, 0, 0)),
                  full(w_abs), full(w_pe), full(w_per), pos, pos],
        out_specs=pl.BlockSpec((None, qb, MLA_HEADS * MLA_KV_RANK), lambda bi, i: (bi, i, 0)),
        out_shape=jax.ShapeDtypeStruct((b, s, MLA_HEADS * MLA_KV_RANK), BF16),
        scratch_shapes=[pltpu.VMEM((rows, 2 * LANES), BF16),
                        pltpu.VMEM((s // qb, rows, qb), F32),
                        pltpu.VMEM((rows, LANES), F32),
                        pltpu.VMEM((rows, LANES), F32),
                        pltpu.VMEM((rows, MLA_KV_RANK), F32)],
        compiler_params=_cparams(("arbitrary", "arbitrary")),
        name="attn",
    )(cq, kc, w_abs, w_pe, w_per, cos4, sin4)


def _ssd_kernel(xbc_ref, dt_ref, z_ref, alog_ref, dfull_ref, norm_ref, e_ref, o_ref, state_scr):
    c = pl.program_id(1)
    ln = xbc_ref.shape[0]
    gw = SSD_INNER // SSD_GROUPS
    hpg = SSD_HEADS // SSD_GROUPS

    @pl.when(c == 0)
    def _():
        state_scr[...] = jnp.zeros(state_scr.shape, F32)

    xs = xbc_ref[:, 0:SSD_INNER].astype(F32)
    dt = dt_ref[...]
    lane1 = lax.broadcasted_iota(jnp.int32, (1, LANES), 1)
    a = jnp.where(lane1 < SSD_HEADS, -jnp.exp(alog_ref[...]), 0.0)
    adt = dt * a
    row = lax.broadcasted_iota(jnp.int32, (ln, ln), 0)
    col = lax.broadcasted_iota(jnp.int32, (ln, ln), 1)
    causal = col <= row
    tril = causal.astype(BF16)
    e = e_ref[...]
    acs = sum(jnp.dot(tril, p, preferred_element_type=F32) for p in _split3(adt))
    acs_e = sum(jnp.dot(p, e, preferred_element_type=F32) for p in _split3(acs))
    dt_e = sum(jnp.dot(p, e, preferred_element_type=F32) for p in _split3(dt))
    acs_end = acs_e[ln - 1:ln, :]
    xdt = xs * dt_e
    x_end = (xdt * jnp.exp(acs_end - acs_e)).astype(BF16)
    eacs = jnp.exp(acs_e)
    chunk_decay = jnp.exp(acs_end)
    acs_t = acs.T
    lane_g = lax.broadcasted_iota(jnp.int32, (ln, gw), 1)

    ys = []
    for g in range(SSD_GROUPS):
        bg = xbc_ref[:, SSD_INNER + g * SSD_STATE:SSD_INNER + (g + 1) * SSD_STATE]
        cg = xbc_ref[:, SSD_INNER + SSD_GROUPS * SSD_STATE + g * SSD_STATE:
                     SSD_INNER + SSD_GROUPS * SSD_STATE + (g + 1) * SSD_STATE]
        cb = lax.dot_general(cg, bg, (((1,), (1,)), ((), ())), preferred_element_type=F32)
        prev = state_scr[g]
        y = jnp.dot(cg, prev.astype(BF16), preferred_element_type=F32) * eacs[:, g * gw:(g + 1) * gw]
        xg = xdt[:, g * gw:(g + 1) * gw]
        for hh in range(hpg):
            h = g * hpg + hh
            seg = acs[:, h:h + 1] - acs_t[h:h + 1, :]
            dec = jnp.exp(jnp.where(causal, seg, -jnp.inf))
            xm = jnp.where(lane_g // SSD_HEADDIM == hh, xg, 0.0).astype(BF16)
            y = y + jnp.dot((cb * dec).astype(BF16), xm, preferred_element_type=F32)
        st = lax.dot_general(bg, x_end[:, g * gw:(g + 1) * gw], (((0,), (0,)), ((), ())),
                             preferred_element_type=F32)
        state_scr[g] = chunk_decay[:, g * gw:(g + 1) * gw] * prev + st
        ys.append(y)

    y = jnp.concatenate(ys, axis=1) + dfull_ref[...] * xs
    zf = z_ref[...].astype(F32)
    yz = y * (zf * _sigmoid(zf))
    outs = []
    for g in range(SSD_GROUPS):
        yg = yz[:, g * gw:(g + 1) * gw]
        outs.append(yg * lax.rsqrt(jnp.mean(yg * yg, axis=-1, keepdims=True) + EPS))
    o_ref[...] = (jnp.concatenate(outs, axis=1) * norm_ref[...]).astype(BF16)


def _ssd(xbc, dt, z, alog_p, d_full, ssd_norm, expand):
    b, s, _ = xbc.shape
    ln = SSD_L
    tok = lambda w: pl.BlockSpec((None, ln, w), lambda bi, c: (bi, c, 0))
    full = lambda a: pl.BlockSpec(a.shape, lambda bi, c: (0,) * a.ndim)
    return pl.pallas_call(
        _ssd_kernel,
        grid=(b, s // ln),
        in_specs=[tok(SSD_CONV_DIM), tok(LANES), tok(SSD_INNER), full(alog_p), full(d_full), full(ssd_norm),
                  full(expand)],
        out_specs=tok(SSD_INNER),
        out_shape=jax.ShapeDtypeStruct((b, s, SSD_INNER), BF16),
        scratch_shapes=[pltpu.VMEM((SSD_GROUPS, SSD_STATE, SSD_INNER // SSD_GROUPS), F32)],
        compiler_params=_cparams(("arbitrary", "arbitrary")),
        name="ssd",
    )(xbc, dt, z, alog_p, d_full, ssd_norm, expand)


def _memkv_kernel(mem_ref, wk_ref, wv_ref, k_ref, v_ref):
    m = mem_ref[...].astype(BF16)
    k_ref[...] = jnp.dot(m, wk_ref[...], preferred_element_type=F32).astype(BF16)
    v_ref[...] = jnp.dot(m, wv_ref[...], preferred_element_type=F32).astype(BF16)


def _memkv(mem, wk, wv):
    b, m, d = mem.shape
    blk = pl.BlockSpec((None, m, d), lambda bi: (bi, 0, 0))
    full = lambda a: pl.BlockSpec(a.shape, lambda bi: (0,) * a.ndim)
    return pl.pallas_call(
        _memkv_kernel,
        grid=(b,),
        in_specs=[blk, full(wk), full(wv)],
        out_specs=(blk, blk),
        out_shape=(jax.ShapeDtypeStruct((b, m, d), BF16), jax.ShapeDtypeStruct((b, m, d), BF16)),
        compiler_params=_cparams(("arbitrary",)),
        name="memkv",
    )(mem, wk, wv)


def _split_bf16(w):
    hi = w.astype(BF16)
    return hi, (w - hi.astype(F32)).astype(BF16)


def _router_logits(h, wrh_ref, wrl_ref, br_ref):
    h_hi, h_lo = _split_bf16(h)
    return (jnp.dot(h_hi, wrh_ref[...], preferred_element_type=F32)
            + jnp.dot(h_hi, wrl_ref[...], preferred_element_type=F32)
            + jnp.dot(h_lo, wrh_ref[...], preferred_element_type=F32)) + br_ref[...]


def _is_group_lane(lane):
    return (lane >= N_EXPERTS) & (lane < N_EXPERTS + N_EXPERT_GROUPS)


def _mid_kernel(x_ref, ol_ref, ss_ref, wmix_ref, g1_ref, b1_ref, wq_ref, km_ref, vm_ref, wo_ref,
                g2_ref, b2_ref, wrh_ref, wrl_ref, br_ref, row_ref, meta_ref, cnt_ref, run_scr):
    step = pl.program_id(0) * pl.num_programs(1) + pl.program_id(1)
    tm = x_ref.shape[0]
    n_lat = MLA_HEADS * MLA_KV_RANK

    @pl.when(step == 0)
    def _():
        run_scr[...] = jnp.zeros(run_scr.shape, F32)

    ns = MID_SUB
    lane = lax.broadcasted_iota(jnp.int32, (ns, LANES), 1)
    is_g = _is_group_lane(lane)
    r = lax.broadcasted_iota(jnp.int32, (ns, ns), 0)
    c = lax.broadcasted_iota(jnp.int32, (ns, ns), 1)
    strictly_before = (c < r).astype(BF16)
    run = run_scr[...]
    for sub in range(tm // ns):
        rs = slice(sub * ns, (sub + 1) * ns)
        mix = (jnp.dot(ol_ref[rs, :], wmix_ref[0:n_lat, :], preferred_element_type=F32)
               + jnp.dot(ss_ref[rs, :], wmix_ref[n_lat:, :], preferred_element_type=F32))
        h1 = _layer_norm(ALPHA * x_ref[rs, :] + mix, g1_ref[...], b1_ref[...])

        q = jnp.dot(h1.astype(BF16), wq_ref[...], preferred_element_type=F32).astype(BF16)
        outs = []
        for h in range(XA_HEADS):
            sl = slice(h * XA_HEAD_DIM, (h + 1) * XA_HEAD_DIM)
            s = lax.dot_general(q[:, sl], km_ref[:, sl], (((1,), (1,)), ((), ())), preferred_element_type=F32)
            p = jnp.exp(s - jnp.max(s, axis=-1, keepdims=True))
            o = jnp.dot(p.astype(BF16), vm_ref[:, sl], preferred_element_type=F32)
            outs.append((o / jnp.sum(p, axis=-1, keepdims=True)).astype(BF16))
        xa = jnp.dot(jnp.concatenate(outs, axis=1), wo_ref[...], preferred_element_type=F32)
        h2 = _layer_norm(ALPHA * h1 + xa, g2_ref[...], b2_ref[...])

        logits = _router_logits(h2, wrh_ref, wrl_ref, br_ref)
        gmax = jnp.max(jnp.where(is_g, logits, -jnp.inf), axis=-1, keepdims=True)
        g_idx = jnp.min(jnp.where(is_g & (logits == gmax), lane - N_EXPERTS, LANES), axis=-1, keepdims=True)
        in_grp = (lane < N_EXPERTS) & (lane // EXPERTS_PER_GROUP == g_idx)
        t1 = jnp.max(jnp.where(in_grp, logits, -jnp.inf), axis=-1, keepdims=True)
        i1 = jnp.min(jnp.where(in_grp & (logits == t1), lane, LANES), axis=-1, keepdims=True)
        rest = in_grp & (lane != i1)
        t2 = jnp.max(jnp.where(rest, logits, -jnp.inf), axis=-1, keepdims=True)
        i2 = jnp.min(jnp.where(rest & (logits == t2), lane, LANES), axis=-1, keepdims=True)
        ea = jnp.minimum(i1, i2) - g_idx * EXPERTS_PER_GROUP
        eb = jnp.maximum(i1, i2) - g_idx * EXPERTS_PER_GROUP
        pair = ea * EXPERTS_PER_GROUP - (ea * (ea + 1)) // 2 + eb - ea - 1
        bucket = g_idx * PAIRS_PER_GROUP + pair
        onehot = (lane == bucket).astype(BF16)
        before = jnp.dot(strictly_before, onehot, preferred_element_type=F32) + run
        rank = jnp.sum(jnp.where(lane == bucket, before, 0.0), axis=-1, keepdims=True)
        run = run + jnp.sum(onehot.astype(F32), axis=0, keepdims=True)
        meta_ref[rs, :] = jnp.where(lane == 0, bucket.astype(F32), jnp.where(lane == 1, rank, 0.0))

        for k in range(ROW_SUB):
            row_ref[pl.ds(sub * ns * ROW_SUB + k, ns, stride=ROW_SUB), :] = h2[:, k * LANES:(k + 1) * LANES]
    run_scr[...] = run
    cnt_ref[...] = run


def _mid(x, o_lat, ssd_out, w_mix, g1, b1, wq, kmem, vmem, wo, g2, b2, wrh, wrl, br):
    b, s, d = x.shape
    tm = TM_MID
    tok = lambda w: pl.BlockSpec((None, tm, w), lambda bi, j: (bi, j, 0))
    full = lambda a: pl.BlockSpec(a.shape, lambda bi, j: (0,) * a.ndim)
    mem = pl.BlockSpec((None,) + kmem.shape[1:], lambda bi, j: (bi, 0, 0))
    tok_rows = lambda n: pl.BlockSpec((None, n * ROW_SUB, LANES), lambda bi, j: (bi, j, 0))
    return pl.pallas_call(
        _mid_kernel,
        grid=(b, s // tm),
        in_specs=[tok(d), tok(o_lat.shape[-1]), tok(SSD_INNER), full(w_mix), full(g1), full(b1), full(wq),
                  mem, mem, full(wo), full(g2), full(b2), full(wrh), full(wrl), full(br)],
        out_specs=(tok_rows(tm), tok(LANES), pl.BlockSpec((1, LANES), lambda bi, j: (0, 0))),
        out_shape=(jax.ShapeDtypeStruct((b, s * ROW_SUB, LANES), F32),
                   jax.ShapeDtypeStruct((b, s, LANES), F32),
                   jax.ShapeDtypeStruct((1, LANES), F32)),
        scratch_shapes=[pltpu.VMEM((1, LANES), F32)],
        compiler_params=_cparams(("arbitrary", "arbitrary")),
        name="mid",
    )(x, o_lat, ssd_out, w_mix, g1, b1, wq, kmem, vmem, wo, g2, b2, wrh, wrl, br)


def _scatter_kernel(dest_ref, ztile_ref, rows_ref, out_hbm, zero_scr, sem, zsem):
    c = pl.program_id(0)
    n = dest_ref.shape[1]
    tile_rows = zero_scr.shape[0]

    @pl.when(c == 0)
    def _():
        zero_scr[...] = jnp.zeros(zero_scr.shape, F32)

        def zero_copy(k):
            start = pl.multiple_of(jnp.maximum(ztile_ref[0, k], 0) * tile_rows, tile_rows)
            return pltpu.make_async_copy(zero_scr, out_hbm.at[pl.ds(start, tile_rows)], zsem)

        def issue_zero(k, carry):
            @pl.when(ztile_ref[0, k] >= 0)
            def _():
                zero_copy(k).start()
            return carry

        def wait_zero(k, carry):
            @pl.when(ztile_ref[0, k] >= 0)
            def _():
                zero_copy(k).wait()
            return carry

        lax.fori_loop(0, ztile_ref.shape[1], issue_zero, 0)
        lax.fori_loop(0, ztile_ref.shape[1], wait_zero, 0)

    def issue(u, carry):
        for p in range(2):
            t = 2 * u + p
            pltpu.make_async_copy(rows_ref.at[_row(t)], out_hbm.at[_row(dest_ref[0, t])], sem).start(priority=p)
        return carry

    lax.fori_loop(0, n // 2, issue, 0)
    pltpu.make_async_copy(rows_ref, out_hbm.at[pl.ds(0, n * ROW_SUB)], sem).wait()


def _scatter_rows(rows, dest3, ztile, n_out, tile):
    t = rows.shape[0] // ROW_SUB
    n = dest3.shape[-1]
    return pl.pallas_call(
        _scatter_kernel,
        grid=(t // n,),
        in_specs=[pl.BlockSpec((None, 1, n), lambda c: (c, 0, 0), memory_space=pltpu.SMEM),
                  pl.BlockSpec(ztile.shape, lambda c: (0, 0), memory_space=pltpu.SMEM),
                  pl.BlockSpec((n * ROW_SUB, LANES), lambda c: (c, 0))],
        out_specs=pl.BlockSpec(memory_space=pl.ANY),
        out_shape=jax.ShapeDtypeStruct((n_out * ROW_SUB, LANES), rows.dtype),
        scratch_shapes=[pltpu.VMEM((tile * ROW_SUB, LANES), F32), pltpu.SemaphoreType.DMA(()),
                        pltpu.SemaphoreType.DMA(())],
        compiler_params=_cparams(("arbitrary",)),
        name="scatter",
    )(dest3, ztile, rows)


def _unsort_kernel(dcur_ref, dnext_ref, rows_hbm, o_ref, buf, sem):
    c = pl.program_id(0)
    n = o_ref.shape[0]
    slot = c % 2

    def issue(dref, sl):
        def body(u, carry):
            for p in range(2):
                t = 2 * u + p
                pltpu.make_async_copy(rows_hbm.at[_row(dref[0, t])], buf.at[sl, _row(t)],
                                      sem.at[sl]).start(priority=p)
            return carry

        lax.fori_loop(0, n // 2, body, 0)

    @pl.when(c == 0)
    def _():
        issue(dcur_ref, 0)

    @pl.when(c + 1 < pl.num_programs(0))
    def _():
        issue(dnext_ref, 1 - slot)

    pltpu.make_async_copy(rows_hbm.at[pl.ds(0, n * ROW_SUB)], buf.at[slot], sem.at[slot]).wait()
    for k in range(ROW_SUB):
        o_ref[:, k * LANES:(k + 1) * LANES] = buf[slot, _lane_block(k, n), :]


def _unsort_rows(rows, dest3, n_tok):
    n = dest3.shape[-1]
    nc = n_tok // n
    return pl.pallas_call(
        _unsort_kernel,
        grid=(nc,),
        in_specs=[pl.BlockSpec((None, 1, n), lambda c: (c, 0, 0), memory_space=pltpu.SMEM),
                  pl.BlockSpec((None, 1, n), lambda c: (jnp.minimum(c + 1, nc - 1), 0, 0),
                               memory_space=pltpu.SMEM),
                  pl.BlockSpec(memory_space=pl.ANY)],
        out_specs=pl.BlockSpec((n, D_MODEL), lambda c: (c, 0)),
        out_shape=jax.ShapeDtypeStruct((n_tok, D_MODEL), rows.dtype),
        scratch_shapes=[pltpu.VMEM((2, n * ROW_SUB, LANES), F32), pltpu.SemaphoreType.DMA((2,))],
        compiler_params=_cparams(("arbitrary",)),
        name="unsort",
    )(dest3, dest3, rows)


def _moe_kernel(e1_ref, e2_ref, nused_ref, rows_ref, wr_ref, br_ref, wg1_ref, wu1_ref, wd1_ref,
                wg2_ref, wu2_ref, wd2_ref, g3_ref, b3_ref, o_ref):
    i = pl.program_id(0)
    tm = rows_ref.shape[0] // ROW_SUB

    @pl.when(i < nused_ref[0])
    def _():
        e1 = e1_ref[i]
        e2 = e2_ref[i]
        g = e1 // EXPERTS_PER_GROUP
        x = jnp.concatenate([rows_ref[_lane_block(k, tm), :] for k in range(ROW_SUB)], axis=1)
        xb = x.astype(BF16)

        logits = jnp.dot(xb, wr_ref[...], preferred_element_type=F32) + br_ref[...]
        lane = lax.broadcasted_iota(jnp.int32, (tm, LANES), 1)
        is_g = _is_group_lane(lane)
        pick = lambda idx: jnp.sum(jnp.where(lane == idx, logits, 0.0), axis=-1, keepdims=True)
        gmax = jnp.max(jnp.where(is_g, logits, -jnp.inf), axis=-1, keepdims=True)
        g_gate = jnp.exp(pick(N_EXPERTS + g) - gmax) / jnp.sum(jnp.where(is_g, jnp.exp(logits - gmax), 0.0),
                                                                axis=-1, keepdims=True)
        t1 = pick(e1)
        t2 = pick(e2)
        c1 = g_gate / (1.0 + jnp.exp(t2 - t1))
        c2 = g_gate / (1.0 + jnp.exp(t1 - t2))

        def expert(wg_ref, wu_ref, wd_ref, ce):
            hg = jnp.dot(xb, wg_ref[...], preferred_element_type=F32)
            hu = jnp.dot(xb, wu_ref[...], preferred_element_type=F32)
            hd = hg * _sigmoid(hg) * hu * ce
            return jnp.dot(hd.astype(BF16), wd_ref[...], preferred_element_type=F32)

        acc = expert(wg1_ref, wu1_ref, wd1_ref, c1) + expert(wg2_ref, wu2_ref, wd2_ref, c2)
        y = _layer_norm(ALPHA * x + acc, g3_ref[...], b3_ref[...])
        for k in range(ROW_SUB):
            o_ref[_lane_block(k, tm), :] = y[:, k * LANES:(k + 1) * LANES]

    @pl.when(i >= nused_ref[0])
    def _():
        o_ref[...] = jnp.zeros(o_ref.shape, F32)


def _moe(rows_sorted, tile_e1, tile_e2, n_used, wr, br, wg, wu, wd, g3, b3):
    n_rows = rows_sorted.shape[0] // ROW_SUB
    tm = TM_MOE
    nt = n_rows // tm

    def tile_map(i, e1, e2, nused):
        return (jnp.maximum(jnp.minimum(i, nused[0] - 1), 0), 0)

    first = lambda i, e1, e2, nused: (e1[i], 0, 0)
    second = lambda i, e1, e2, nused: (e2[i], 0, 0)
    full = lambda a: pl.BlockSpec(a.shape, lambda i, e1, e2, nused: (0,) * a.ndim)
    w_in = lambda m: pl.BlockSpec((None, D_MODEL, D_EXPERT), m)
    w_out = lambda m: pl.BlockSpec((None, D_EXPERT, D_MODEL), m)
    grid_spec = pltpu.PrefetchScalarGridSpec(
        num_scalar_prefetch=3,
        grid=(nt,),
        in_specs=[pl.BlockSpec((tm * ROW_SUB, LANES), tile_map), full(wr), full(br),
                  w_in(first), w_in(first), w_out(first), w_in(second), w_in(second), w_out(second),
                  full(g3), full(b3)],
        out_specs=pl.BlockSpec((tm * ROW_SUB, LANES), lambda i, e1, e2, nused: (i, 0)),
    )
    return pl.pallas_call(
        _moe_kernel,
        grid_spec=grid_spec,
        out_shape=jax.ShapeDtypeStruct((n_rows * ROW_SUB, LANES), F32),
        compiler_params=_cparams(("arbitrary",)),
        name="moe",
    )(tile_e1, tile_e2, n_used, rows_sorted, wr, br, wg, wu, wd, wg, wu, wd, g3, b3)


def _rope_tables(seq):
    pos = jnp.arange(seq, dtype=F32)
    inv_freq = ROPE_THETA ** (-jnp.arange(0, MLA_ROPE, 2, dtype=F32) / MLA_ROPE)
    ang = pos[:, None] * inv_freq[None, :]
    cos = jnp.tile(jnp.cos(ang), (1, LANES // (MLA_ROPE // 2)))
    sin = jnp.tile(jnp.sin(ang), (1, LANES // (MLA_ROPE // 2)))
    return cos, sin


def _pad_lanes(v, n=LANES):
    v = v.reshape(1, -1)
    return jnp.pad(v, ((0, 0), (0, n - v.shape[1])))


def kernel(x, mem, w_in, mla_q_norm, w_q_up, mla_kv_norm, w_kv_up, ssd_conv_w, ssd_conv_b, ssd_dt_bias, ssd_a_log, ssd_d, ssd_norm, w_out, ln1_g, ln1_b, xa_wq, xa_wk, xa_wv, xa_wo, ln2_g, ln2_b, router_group_w, router_group_b, router_expert_w, router_expert_b, expert_w_gate, expert_w_up, expert_w_down, ln3_g, ln3_b):
    b, s, d = x.shape
    t = b * s
    l = 0
    row = lambda v: v[l].reshape(1, -1)

    wi = w_in[l]
    o1 = MLA_Q_RANK
    o2 = o1 + MLA_KV_RANK
    o3 = o2 + MLA_ROPE
    o4 = o3 + SSD_INNER
    o5 = o4 + SSD_CONV_DIM
    w_kr = wi[:, o2:o3]
    half = MLA_ROPE // 2
    w_kr_rot = jnp.concatenate([-w_kr[:, half:], w_kr[:, :half]], axis=1)
    reps = LANES // MLA_ROPE
    w_in_r = jnp.concatenate(
        [wi[:, :o2], wi[:, o3:o4], wi[:, o4:o5], jnp.tile(w_kr, (1, reps)), jnp.tile(w_kr_rot, (1, reps)),
         wi[:, o5:], jnp.zeros((d, LANES - SSD_HEADS), F32)], axis=1).astype(BF16)
    cos4, sin4 = _rope_tables(s)
    attn_scale = (MLA_NOPE + MLA_ROPE) ** -0.5 * math.log2(math.e)
    w_abs, w_pe, w_per, w_mix = _fold_weights(w_q_up[l], w_kv_up[l], mla_kv_norm[l], w_out[l], attn_scale)
    expand = (jnp.arange(LANES)[:, None] == (jnp.arange(SSD_INNER)[None, :] // SSD_HEADDIM)).astype(BF16)
    d_full = jnp.repeat(ssd_d[l], SSD_HEADDIM).reshape(1, SSD_INNER)
    wrh, wrl = _split_bf16(jnp.concatenate(
        [router_expert_w[l], router_group_w[l], jnp.zeros((d, LANES - N_EXPERTS - N_EXPERT_GROUPS), F32)], axis=1))
    br = _pad_lanes(jnp.concatenate([router_expert_b[l], router_group_b[l]]))

    cq, kc, z, xbc, dt = _inproj(x, w_in_r, row(mla_q_norm), ssd_conv_w[l], row(ssd_conv_b),
                                 _pad_lanes(ssd_dt_bias[l]), cos4, sin4)
    o_lat = _attention(cq, kc, w_abs, w_pe, w_per, cos4, sin4)
    ssd_out = _ssd(xbc, dt, z, _pad_lanes(ssd_a_log[l]), d_full, row(ssd_norm), expand)

    kmem, vmem = _memkv(mem, xa_wk[l].astype(BF16), xa_wv[l].astype(BF16))
    wq = (xa_wq[l] * (XA_HEAD_DIM ** -0.5)).astype(BF16)
    rows, meta, counts = _mid(x, o_lat, ssd_out, w_mix, row(ln1_g), row(ln1_b), wq, kmem, vmem,
                              xa_wo[l].astype(BF16), row(ln2_g), row(ln2_b), wrh, wrl, br)

    tm = TM_MOE
    nt = t // tm + N_EXPERT_GROUPS * PAIRS_PER_GROUP - 1
    meta = meta.reshape(t, LANES)
    bucket = meta[:, 0].astype(jnp.int32)
    rank = meta[:, 1].astype(jnp.int32)
    cnt = counts[0].astype(jnp.int32)
    ntile = (cnt + tm - 1) // tm
    tile_end = jnp.cumsum(ntile)
    tile_start = tile_end - ntile
    first_row = jnp.sum(jnp.where(bucket[:, None] == jnp.arange(N_BUCKETS)[None, :], tile_start[None, :] * tm, 0),
                        axis=1)
    dest = first_row + rank
    n_used = tile_end[-1:].astype(jnp.int32)
    tile_bkt = jnp.minimum(jnp.sum(jnp.arange(nt)[:, None] >= tile_end[None, :], axis=1), N_BUCKETS - 1)
    pa, pb = zip(*[(a, c) for a in range(EXPERTS_PER_GROUP) for c in range(a + 1, EXPERTS_PER_GROUP)])
    bkt_g = jnp.minimum(tile_bkt // PAIRS_PER_GROUP, N_EXPERT_GROUPS - 1)
    tile_e1 = (bkt_g * EXPERTS_PER_GROUP + jnp.array(pa)[tile_bkt % PAIRS_PER_GROUP]).astype(jnp.int32)
    tile_e2 = (bkt_g * EXPERTS_PER_GROUP + jnp.array(pb)[tile_bkt % PAIRS_PER_GROUP]).astype(jnp.int32)
    spare = n_used[0] + jnp.arange(nt - t // tm)
    ztile = jnp.concatenate([jnp.where(ntile > 0, tile_end - 1, -1),
                             jnp.where(spare < nt, spare, -1)]).astype(jnp.int32).reshape(1, -1)
    sc_chunk = min(SCATTER_CHUNK, t)
    un_chunk = min(UNSORT_CHUNK, t)

    rows_sorted = _scatter_rows(rows.reshape(t * ROW_SUB, LANES), dest.reshape(t // sc_chunk, 1, sc_chunk),
                                ztile, nt * tm, tm)
    out_sorted = _moe(rows_sorted, tile_e1, tile_e2, n_used, wrh, br, expert_w_gate[l].astype(BF16),
                      expert_w_up[l].astype(BF16), expert_w_down[l].astype(BF16), row(ln3_g), row(ln3_b))
    out = _unsort_rows(out_sorted, dest.reshape(t // un_chunk, 1, un_chunk), t)
    return out.reshape(b, s, d)
```

```python
import functools
import math

import jax
import jax.numpy as jnp
from jax import lax
from jax.experimental import pallas as pl
from jax.experimental.pallas import tpu as pltpu

F32 = jnp.float32
BF16 = jnp.bfloat16
HIGHEST = lax.Precision.HIGHEST

EPS = 1e-5
NEG_INF = -1e30
CHUNK = 64
ROPE_THETA = 10000.0

D_MODEL = 1024
MLA_HEADS = 8
MLA_NOPE = 64
MLA_ROPE = 32
MLA_V = 64
MLA_Q_RANK = 256
MLA_KV_RANK = 128
SSD_HEADS = 8
SSD_HEADDIM = 64
SSD_INNER = 512
SSD_GROUPS = 2
SSD_STATE = 128
SSD_CONV = 4
SSD_CONV_DIM = 1024
XA_HEADS = 4
XA_HEAD_DIM = 256
N_EXPERT_GROUPS = 4
EXPERTS_PER_GROUP = 8
N_EXPERTS = 32
D_EXPERT = 256
DEPTH = 1
ALPHA = (2.0 * DEPTH) ** 0.25

LANES = 128
V7X_VMEM_LIMIT = 56 * 1024 * 1024

C_CQ = 0
C_CKV = 256
C_Z = 384
C_XBC = 896
C_KPE = 1920
C_KPER = 2048
C_DT = 2176
IN_COLS_R = 2304

TM_IN = 1024
Q_BLK = 256
SSD_L = 256
TM_MID = 1024
MID_SUB = 1024
TM_MOE = 256
PAIRS_PER_GROUP = EXPERTS_PER_GROUP * (EXPERTS_PER_GROUP - 1) // 2
N_BUCKETS = LANES
SCATTER_CHUNK = 4096
UNSORT_CHUNK = 2048
ROW_SUB = D_MODEL // LANES


def _cparams(sem):
    return pltpu.CompilerParams(dimension_semantics=sem, vmem_limit_bytes=V7X_VMEM_LIMIT)


def _sigmoid(x):
    return 1.0 / (1.0 + jnp.exp(-x))


def _split3(x):
    hi = x.astype(BF16)
    r = x - hi.astype(F32)
    mid = r.astype(BF16)
    return hi, mid, (r - mid.astype(F32)).astype(BF16)


def _row(t):
    return pl.ds(pl.multiple_of(t * ROW_SUB, ROW_SUB), ROW_SUB)


def _lane_block(k, n):
    return pl.ds(k, n, stride=ROW_SUB)


def _layer_norm(x, g, b):
    mu = jnp.mean(x, axis=-1, keepdims=True)
    xc = x - mu
    var = jnp.mean(xc * xc, axis=-1, keepdims=True)
    return xc * lax.rsqrt(var + EPS) * g + b


def _fold_kernel(wqn_ref, wukt_ref, wuv_ref, woa_ref, gkv_row_ref, gkv_col_ref, wabs_ref, wof_ref, *, scale):
    for h in range(MLA_HEADS):
        wabs_ref[h] = scale * jnp.dot(wqn_ref[h], wukt_ref[h] * gkv_row_ref[...],
                                      precision=HIGHEST, preferred_element_type=F32)
        wof_ref[h] = jnp.dot(wuv_ref[h] * gkv_col_ref[...], woa_ref[h],
                             precision=HIGHEST, preferred_element_type=F32)


def _fold_weights(w_q_up, w_kv_up, mla_kv_norm, w_out, scale):
    wq = w_q_up.reshape(MLA_Q_RANK, MLA_HEADS, MLA_NOPE + MLA_ROPE)
    wkv = w_kv_up.reshape(MLA_KV_RANK, MLA_HEADS, MLA_NOPE + MLA_V)
    wqn = jnp.transpose(wq[:, :, :MLA_NOPE], (1, 0, 2))
    wukt = jnp.transpose(wkv[:, :, :MLA_NOPE], (1, 2, 0))
    wuv = jnp.transpose(wkv[:, :, MLA_NOPE:], (1, 0, 2))
    woa = w_out[:MLA_HEADS * MLA_V].reshape(MLA_HEADS, MLA_V, D_MODEL)
    wabs, wof = pl.pallas_call(
        functools.partial(_fold_kernel, scale=scale),
        out_shape=(jax.ShapeDtypeStruct((MLA_HEADS, MLA_Q_RANK, MLA_KV_RANK), F32),
                   jax.ShapeDtypeStruct((MLA_HEADS, MLA_KV_RANK, D_MODEL), F32)),
        name="fold",
    )(wqn, wukt, wuv, woa, mla_kv_norm.reshape(1, MLA_KV_RANK), mla_kv_norm.reshape(MLA_KV_RANK, 1))
    w_abs = jnp.transpose(wabs, (1, 0, 2)).reshape(MLA_Q_RANK, MLA_HEADS * MLA_KV_RANK).astype(BF16)
    w_pe3 = wq[:, :, MLA_NOPE:] * scale
    half = MLA_ROPE // 2
    w_per3 = jnp.concatenate([-w_pe3[:, :, half:], w_pe3[:, :, :half]], axis=-1)
    w_pe = w_pe3.reshape(MLA_Q_RANK, MLA_HEADS * MLA_ROPE).astype(BF16)
    w_per = w_per3.reshape(MLA_Q_RANK, MLA_HEADS * MLA_ROPE).astype(BF16)
    w_mix = jnp.concatenate([wof.reshape(MLA_HEADS * MLA_KV_RANK, D_MODEL), w_out[MLA_HEADS * MLA_V:]],
                            axis=0).astype(BF16)
    return w_abs, w_pe, w_per, w_mix


def _inproj_kernel(x_ref, w_ref, gq_ref, cw_ref, cb_ref, dtb_ref, cos_ref, sin_ref,
                   cq_ref, kc_ref, z_ref, xbc_ref, dt_ref, cbuf):
    j = pl.program_id(1)
    tm = x_ref.shape[0]
    @pl.when(j == 0)
    def _():
        cbuf[0:8, :] = jnp.zeros((8, SSD_CONV_DIM), F32)

    xb = x_ref[...].astype(BF16)
    seg = lambda lo, n: jnp.dot(xb, w_ref[:, lo:lo + n], preferred_element_type=F32)

    u = seg(C_XBC, SSD_CONV_DIM)
    cbuf[8:8 + tm, :] = u
    acc = cb_ref[...] + cw_ref[SSD_CONV - 1:SSD_CONV, :] * u
    for k in range(SSD_CONV - 1):
        acc = acc + cw_ref[k:k + 1, :] * cbuf[pl.ds(8 - (SSD_CONV - 1) + k, tm), :]
    xbc_ref[...] = (acc * _sigmoid(acc)).astype(BF16)
    cbuf[0:8, :] = cbuf[tm:tm + 8, :]

    z_ref[...] = seg(C_Z, SSD_INNER).astype(BF16)

    tail = seg(C_KPE, 3 * LANES)
    dtr = tail[:, 2 * LANES:] + dtb_ref[...]
    dt_ref[...] = jnp.maximum(dtr, 0.0) + jnp.log(1.0 + jnp.exp(-jnp.abs(dtr)))
    kpe = tail[:, :LANES] * cos_ref[...] + tail[:, LANES:2 * LANES] * sin_ref[...]

    lat = seg(C_CQ, MLA_Q_RANK + MLA_KV_RANK)
    c_q = lat[:, :MLA_Q_RANK]
    cq = c_q * lax.rsqrt(jnp.mean(c_q * c_q, axis=-1, keepdims=True) + EPS) * gq_ref[...]
    cq_ref[...] = cq.astype(BF16)
    c_kv = lat[:, MLA_Q_RANK:]
    ckv = c_kv * lax.rsqrt(jnp.mean(c_kv * c_kv, axis=-1, keepdims=True) + EPS)
    kc_ref[...] = jnp.concatenate([ckv, kpe], axis=1).astype(BF16)


def _inproj(x, w_in_r, gq, conv_w, conv_b, dt_bias_p, cos4, sin4):
    b, s, d = x.shape
    tm = TM_IN
    grid = (b, s // tm)
    tok = lambda w: pl.BlockSpec((None, tm, w), lambda bi, j: (bi, j, 0))
    full = lambda a: pl.BlockSpec(a.shape, lambda bi, j: (0,) * a.ndim)
    pos = pl.BlockSpec((tm, LANES), lambda bi, j: (j, 0))
    return pl.pallas_call(
        _inproj_kernel,
        grid=grid,
        in_specs=[tok(d), full(w_in_r), full(gq), full(conv_w), full(conv_b), full(dt_bias_p), pos, pos],
        out_specs=(tok(MLA_Q_RANK), tok(2 * LANES), tok(SSD_INNER), tok(SSD_CONV_DIM), tok(LANES)),
        out_shape=(jax.ShapeDtypeStruct((b, s, MLA_Q_RANK), BF16),
                   jax.ShapeDtypeStruct((b, s, 2 * LANES), BF16),
                   jax.ShapeDtypeStruct((b, s, SSD_INNER), BF16),
                   jax.ShapeDtypeStruct((b, s, SSD_CONV_DIM), BF16),
                   jax.ShapeDtypeStruct((b, s, LANES), F32)),
        scratch_shapes=[pltpu.VMEM((tm + 8, SSD_CONV_DIM), F32)],
        compiler_params=_cparams(("arbitrary", "arbitrary")),
        name="inproj",
    )(x, w_in_r, gq, conv_w, conv_b, dt_bias_p, cos4, sin4)


def _attn_kernel(cq_ref, kc_ref, wabs_ref, wpe_ref, wper_ref, cos_ref, sin_ref, o_ref,
                 q_scr, s_scr, mx_scr, ls_scr, acc_scr):
    i = pl.program_id(1)
    qb = cq_ref.shape[0]
    rows = MLA_HEADS * qb
    cq = cq_ref[...]
    qabs = jnp.dot(cq, wabs_ref[...], preferred_element_type=F32)
    cos8 = jnp.concatenate([cos_ref[...], cos_ref[...]], axis=1)
    sin8 = jnp.concatenate([sin_ref[...], sin_ref[...]], axis=1)
    qpe = (jnp.dot(cq, wpe_ref[...], preferred_element_type=F32) * cos8
           + jnp.dot(cq, wper_ref[...], preferred_element_type=F32) * sin8)
    lane = lax.broadcasted_iota(jnp.int32, (qb, LANES), 1)
    heads_per_blk = LANES // MLA_ROPE
    for h in range(MLA_HEADS):
        pe_blk = qpe[:, LANES * (h // heads_per_blk):LANES * (h // heads_per_blk + 1)]
        pe_h = jnp.where(lane // MLA_ROPE == h % heads_per_blk, pe_blk, 0.0)
        q_scr[h * qb:(h + 1) * qb, :] = jnp.concatenate(
            [qabs[:, h * MLA_KV_RANK:(h + 1) * MLA_KV_RANK], pe_h], axis=1).astype(BF16)

    def scores(j):
        kblk = kc_ref[pl.ds(pl.multiple_of(j * qb, qb), qb), :]
        return lax.dot_general(q_scr[...], kblk, (((1,), (1,)), ((), ())), preferred_element_type=F32)

    def lane_max(s):
        return jnp.maximum(s[:, :LANES], s[:, LANES:])

    mx_scr[...] = jnp.full((rows, LANES), -jnp.inf, F32)

    def pass1(j, carry):
        s = scores(j)
        s_scr[j] = s
        mx_scr[...] = jnp.maximum(mx_scr[...], lane_max(s))
        return carry

    lax.fori_loop(0, i, pass1, 0)
    r = lax.broadcasted_iota(jnp.int32, (qb, qb), 0)
    c = lax.broadcasted_iota(jnp.int32, (qb, qb), 1)
    visible = c // CHUNK <= r // CHUNK
    s = scores(i)
    s = jnp.concatenate([jnp.where(visible, s[h * qb:(h + 1) * qb], NEG_INF) for h in range(MLA_HEADS)],
                        axis=0)
    s_scr[i] = s
    m = jnp.max(jnp.maximum(mx_scr[...], lane_max(s)), axis=-1, keepdims=True)
    mx_scr[...] = jnp.broadcast_to(m, (rows, LANES))

    ls_scr[...] = jnp.zeros((rows, LANES), F32)
    acc_scr[...] = jnp.zeros((rows, MLA_KV_RANK), F32)

    def pass2(j, carry):
        sj = s_scr[j]
        mb = mx_scr[...]
        p = jnp.concatenate([jnp.exp2(sj[:, :LANES] - mb), jnp.exp2(sj[:, LANES:] - mb)], axis=1)
        ls_scr[...] = ls_scr[...] + p[:, :LANES] + p[:, LANES:]
        vblk = kc_ref[pl.ds(pl.multiple_of(j * qb, qb), qb), 0:MLA_KV_RANK]
        acc_scr[...] = acc_scr[...] + jnp.dot(p.astype(BF16), vblk, preferred_element_type=F32)
        return carry

    lax.fori_loop(0, i + 1, pass2, 0)

    o = acc_scr[...] * (1.0 / jnp.sum(ls_scr[...], axis=-1, keepdims=True))
    for h in range(MLA_HEADS):
        o_ref[:, h * MLA_KV_RANK:(h + 1) * MLA_KV_RANK] = o[h * qb:(h + 1) * qb, :].astype(BF16)


def _attention(cq, kc, w_abs, w_pe, w_per, cos4, sin4):
    b, s, _ = cq.shape
    qb = Q_BLK
    rows = MLA_HEADS * qb
    full = lambda a: pl.BlockSpec(a.shape, lambda bi, i: (0,) * a.ndim)
    pos = pl.BlockSpec((qb, LANES), lambda bi, i: (i, 0))
    return pl.pallas_call(
        _attn_kernel,
        grid=(b, s // qb),
        in_specs=[pl.BlockSpec((None, qb, MLA_Q_RANK), lambda bi, i: (bi, i, 0)),
                  pl.BlockSpec((None, s, 2 * LANES), lambda bi, i: (bi, 0, 0)),
                  full(w_abs), full(w_pe), full(w_per), pos, pos],
        out_specs=pl.BlockSpec((None, qb, MLA_HEADS * MLA_KV_RANK), lambda bi, i: (bi, i, 0)),
        out_shape=jax.ShapeDtypeStruct((b, s, MLA_HEADS * MLA_KV_RANK), BF16),
        scratch_shapes=[pltpu.VMEM((rows, 2 * LANES), BF16),
                        pltpu.VMEM((s // qb, rows, qb), F32),
                        pltpu.VMEM((rows, LANES), F32),
                        pltpu.VMEM((rows, LANES), F32),
                        pltpu.VMEM((rows, MLA_KV_RANK), F32)],
        compiler_params=_cparams(("arbitrary", "arbitrary")),
        name="attn",
    )(cq, kc, w_abs, w_pe, w_per, cos4, sin4)


def _ssd_kernel(xbc_ref, dt_ref, z_ref, alog_ref, dfull_ref, norm_ref, e_ref, o_ref, state_scr):
    c = pl.program_id(1)
    ln = xbc_ref.shape[0]
    gw = SSD_INNER // SSD_GROUPS
    hpg = SSD_HEADS // SSD_GROUPS

    @pl.when(c == 0)
    def _():
        state_scr[...] = jnp.zeros(state_scr.shape, F32)

    xs = xbc_ref[:, 0:SSD_INNER].astype(F32)
    dt = dt_ref[...]
    lane1 = lax.broadcasted_iota(jnp.int32, (1, LANES), 1)
    a = jnp.where(lane1 < SSD_HEADS, -jnp.exp(alog_ref[...]), 0.0)
    adt = dt * a
    row = lax.broadcasted_iota(jnp.int32, (ln, ln), 0)
    col = lax.broadcasted_iota(jnp.int32, (ln, ln), 1)
    causal = col <= row
    tril = causal.astype(BF16)
    e = e_ref[...]
    acs = sum(jnp.dot(tril, p, preferred_element_type=F32) for p in _split3(adt))
    acs_e = sum(jnp.dot(p, e, preferred_element_type=F32) for p in _split3(acs))
    dt_e = sum(jnp.dot(p, e, preferred_element_type=F32) for p in _split3(dt))
    acs_end = acs_e[ln - 1:ln, :]
    xdt = xs * dt_e
    x_end = (xdt * jnp.exp(acs_end - acs_e)).astype(BF16)
    eacs = jnp.exp(acs_e)
    chunk_decay = jnp.exp(acs_end)
    acs_t = acs.T
    lane_g = lax.broadcasted_iota(jnp.int32, (ln, gw), 1)

    ys = []
    for g in range(SSD_GROUPS):
        bg = xbc_ref[:, SSD_INNER + g * SSD_STATE:SSD_INNER + (g + 1) * SSD_STATE]
        cg = xbc_ref[:, SSD_INNER + SSD_GROUPS * SSD_STATE + g * SSD_STATE:
                     SSD_INNER + SSD_GROUPS * SSD_STATE + (g + 1) * SSD_STATE]
        cb = lax.dot_general(cg, bg, (((1,), (1,)), ((), ())), preferred_element_type=F32)
        prev = state_scr[g]
        y = jnp.dot(cg, prev.astype(BF16), preferred_element_type=F32) * eacs[:, g * gw:(g + 1) * gw]
        xg = xdt[:, g * gw:(g + 1) * gw]
        for hh in range(hpg):
            h = g * hpg + hh
            seg = acs[:, h:h + 1] - acs_t[h:h + 1, :]
            dec = jnp.exp(jnp.where(causal, seg, -jnp.inf))
            xm = jnp.where(lane_g // SSD_HEADDIM == hh, xg, 0.0).astype(BF16)
            y = y + jnp.dot((cb * dec).astype(BF16), xm, preferred_element_type=F32)
        st = lax.dot_general(bg, x_end[:, g * gw:(g + 1) * gw], (((0,), (0,)), ((), ())),
                             preferred_element_type=F32)
        state_scr[g] = chunk_decay[:, g * gw:(g + 1) * gw] * prev + st
        ys.append(y)

    y = jnp.concatenate(ys, axis=1) + dfull_ref[...] * xs
    zf = z_ref[...].astype(F32)
    yz = y * (zf * _sigmoid(zf))
    outs = []
    for g in range(SSD_GROUPS):
        yg = yz[:, g * gw:(g + 1) * gw]
        outs.append(yg * lax.rsqrt(jnp.mean(yg * yg, axis=-1, keepdims=True) + EPS))
    o_ref[...] = (jnp.concatenate(outs, axis=1) * norm_ref[...]).astype(BF16)


def _ssd(xbc, dt, z, alog_p, d_full, ssd_norm, expand):
    b, s, _ = xbc.shape
    ln = SSD_L
    tok = lambda w: pl.BlockSpec((None, ln, w), lambda bi, c: (bi, c, 0))
    full = lambda a: pl.BlockSpec(a.shape, lambda bi, c: (0,) * a.ndim)
    return pl.pallas_call(
        _ssd_kernel,
        grid=(b, s // ln),
        in_specs=[tok(SSD_CONV_DIM), tok(LANES), tok(SSD_INNER), full(alog_p), full(d_full), full(ssd_norm),
                  full(expand)],
        out_specs=tok(SSD_INNER),
        out_shape=jax.ShapeDtypeStruct((b, s, SSD_INNER), BF16),
        scratch_shapes=[pltpu.VMEM((SSD_GROUPS, SSD_STATE, SSD_INNER // SSD_GROUPS), F32)],
        compiler_params=_cparams(("arbitrary", "arbitrary")),
        name="ssd",
    )(xbc, dt, z, alog_p, d_full, ssd_norm, expand)


def _memkv_kernel(mem_ref, wk_ref, wv_ref, k_ref, v_ref):
    m = mem_ref[...].astype(BF16)
    k_ref[...] = jnp.dot(m, wk_ref[...], preferred_element_type=F32).astype(BF16)
    v_ref[...] = jnp.dot(m, wv_ref[...], preferred_element_type=F32).astype(BF16)


def _memkv(mem, wk, wv):
    b, m, d = mem.shape
    blk = pl.BlockSpec((None, m, d), lambda bi: (bi, 0, 0))
    full = lambda a: pl.BlockSpec(a.shape, lambda bi: (0,) * a.ndim)
    return pl.pallas_call(
        _memkv_kernel,
        grid=(b,),
        in_specs=[blk, full(wk), full(wv)],
        out_specs=(blk, blk),
        out_shape=(jax.ShapeDtypeStruct((b, m, d), BF16), jax.ShapeDtypeStruct((b, m, d), BF16)),
        compiler_params=_cparams(("arbitrary",)),
        name="memkv",
    )(mem, wk, wv)


def _split_bf16(w):
    hi = w.astype(BF16)
    return hi, (w - hi.astype(F32)).astype(BF16)


def _router_logits(h, wrh_ref, wrl_ref, br_ref):
    h_hi, h_lo = _split_bf16(h)
    return (jnp.dot(h_hi, wrh_ref[...], preferred_element_type=F32)
            + jnp.dot(h_hi, wrl_ref[...], preferred_element_type=F32)
            + jnp.dot(h_lo, wrh_ref[...], preferred_element_type=F32)) + br_ref[...]


def _is_group_lane(lane):
    return (lane >= N_EXPERTS) & (lane < N_EXPERTS + N_EXPERT_GROUPS)


def _mid_kernel(x_ref, ol_ref, ss_ref, wmix_ref, g1_ref, b1_ref, wq_ref, km_ref, vm_ref, wo_ref,
                g2_ref, b2_ref, wrh_ref, wrl_ref, br_ref, row_ref, meta_ref, cnt_ref, run_scr):
    step = pl.program_id(0) * pl.num_programs(1) + pl.program_id(1)
    tm = x_ref.shape[0]
    n_lat = MLA_HEADS * MLA_KV_RANK

    @pl.when(step == 0)
    def _():
        run_scr[...] = jnp.zeros(run_scr.shape, F32)

    ns = MID_SUB
    lane = lax.broadcasted_iota(jnp.int32, (ns, LANES), 1)
    is_g = _is_group_lane(lane)
    r = lax.broadcasted_iota(jnp.int32, (ns, ns), 0)
    c = lax.broadcasted_iota(jnp.int32, (ns, ns), 1)
    strictly_before = (c < r).astype(BF16)
    run = run_scr[...]
    for sub in range(tm // ns):
        rs = slice(sub * ns, (sub + 1) * ns)
        mix = (jnp.dot(ol_ref[rs, :], wmix_ref[0:n_lat, :], preferred_element_type=F32)
               + jnp.dot(ss_ref[rs, :], wmix_ref[n_lat:, :], preferred_element_type=F32))
        h1 = _layer_norm(ALPHA * x_ref[rs, :] + mix, g1_ref[...], b1_ref[...])

        q = jnp.dot(h1.astype(BF16), wq_ref[...], preferred_element_type=F32).astype(BF16)
        outs = []
        for h in range(XA_HEADS):
            sl = slice(h * XA_HEAD_DIM, (h + 1) * XA_HEAD_DIM)
            s = lax.dot_general(q[:, sl], km_ref[:, sl], (((1,), (1,)), ((), ())), preferred_element_type=F32)
            p = jnp.exp(s - jnp.max(s, axis=-1, keepdims=True))
            o = jnp.dot(p.astype(BF16), vm_ref[:, sl], preferred_element_type=F32)
            outs.append((o / jnp.sum(p, axis=-1, keepdims=True)).astype(BF16))
        xa = jnp.dot(jnp.concatenate(outs, axis=1), wo_ref[...], preferred_element_type=F32)
        h2 = _layer_norm(ALPHA * h1 + xa, g2_ref[...], b2_ref[...])

        logits = _router_logits(h2, wrh_ref, wrl_ref, br_ref)
        gmax = jnp.max(jnp.where(is_g, logits, -jnp.inf), axis=-1, keepdims=True)
        g_idx = jnp.min(jnp.where(is_g & (logits == gmax), lane - N_EXPERTS, LANES), axis=-1, keepdims=True)
        in_grp = (lane < N_EXPERTS) & (lane // EXPERTS_PER_GROUP == g_idx)
        t1 = jnp.max(jnp.where(in_grp, logits, -jnp.inf), axis=-1, keepdims=True)
        i1 = jnp.min(jnp.where(in_grp & (logits == t1), lane, LANES), axis=-1, keepdims=True)
        rest = in_grp & (lane != i1)
        t2 = jnp.max(jnp.where(rest, logits, -jnp.inf), axis=-1, keepdims=True)
        i2 = jnp.min(jnp.where(rest & (logits == t2), lane, LANES), axis=-1, keepdims=True)
        ea = jnp.minimum(i1, i2) - g_idx * EXPERTS_PER_GROUP
        eb = jnp.maximum(i1, i2) - g_idx * EXPERTS_PER_GROUP
        pair = ea * EXPERTS_PER_GROUP - (ea * (ea + 1)) // 2 + eb - ea - 1
        bucket = g_idx * PAIRS_PER_GROUP + pair
        onehot = (lane == bucket).astype(BF16)
        before = jnp.dot(strictly_before, onehot, preferred_element_type=F32) + run
        rank = jnp.sum(jnp.where(lane == bucket, before, 0.0), axis=-1, keepdims=True)
        run = run + jnp.sum(onehot.astype(F32), axis=0, keepdims=True)
        meta_ref[rs, :] = jnp.where(lane == 0, bucket.astype(F32), jnp.where(lane == 1, rank, 0.0))

        for k in range(ROW_SUB):
            row_ref[pl.ds(sub * ns * ROW_SUB + k, ns, stride=ROW_SUB), :] = h2[:, k * LANES:(k + 1) * LANES]
    run_scr[...] = run
    cnt_ref[...] = run


def _mid(x, o_lat, ssd_out, w_mix, g1, b1, wq, kmem, vmem, wo, g2, b2, wrh, wrl, br):
    b, s, d = x.shape
    tm = TM_MID
    tok = lambda w: pl.BlockSpec((None, tm, w), lambda bi, j: (bi, j, 0))
    full = lambda a: pl.BlockSpec(a.shape, lambda bi, j: (0,) * a.ndim)
    mem = pl.BlockSpec((None,) + kmem.shape[1:], lambda bi, j: (bi, 0, 0))
    tok_rows = lambda n: pl.BlockSpec((None, n * ROW_SUB, LANES), lambda bi, j: (bi, j, 0))
    return pl.pallas_call(
        _mid_kernel,
        grid=(b, s // tm),
        in_specs=[tok(d), tok(o_lat.shape[-1]), tok(SSD_INNER), full(w_mix), full(g1), full(b1), full(wq),
                  mem, mem, full(wo), full(g2), full(b2), full(wrh), full(wrl), full(br)],
        out_specs=(tok_rows(tm), tok(LANES), pl.BlockSpec((1, LANES), lambda bi, j: (0, 0))),
        out_shape=(jax.ShapeDtypeStruct((b, s * ROW_SUB, LANES), F32),
                   jax.ShapeDtypeStruct((b, s, LANES), F32),
                   jax.ShapeDtypeStruct((1, LANES), F32)),
        scratch_shapes=[pltpu.VMEM((1, LANES), F32)],
        compiler_params=_cparams(("arbitrary", "arbitrary")),
        name="mid",
    )(x, o_lat, ssd_out, w_mix, g1, b1, wq, kmem, vmem, wo, g2, b2, wrh, wrl, br)


def _scatter_kernel(dest_ref, ztile_ref, rows_ref, out_hbm, zero_scr, sem, zsem):
    c = pl.program_id(0)
    n = dest_ref.shape[1]
    tile_rows = zero_scr.shape[0]

    @pl.when(c == 0)
    def _():
        zero_scr[...] = jnp.zeros(zero_scr.shape, F32)

        def zero_copy(k):
            start = pl.multiple_of(jnp.maximum(ztile_ref[0, k], 0) * tile_rows, tile_rows)
            return pltpu.make_async_copy(zero_scr, out_hbm.at[pl.ds(start, tile_rows)], zsem)

        def issue_zero(k, carry):
            @pl.when(ztile_ref[0, k] >= 0)
            def _():
                zero_copy(k).start()
            return carry

        def wait_zero(k, carry):
            @pl.when(ztile_ref[0, k] >= 0)
            def _():
                zero_copy(k).wait()
            return carry

        lax.fori_loop(0, ztile_ref.shape[1], issue_zero, 0)
        lax.fori_loop(0, ztile_ref.shape[1], wait_zero, 0)

    def issue(u, carry):
        for p in range(2):
            t = 2 * u + p
            pltpu.make_async_copy(rows_ref.at[_row(t)], out_hbm.at[_row(dest_ref[0, t])], sem).start(priority=p)
        return carry

    lax.fori_loop(0, n // 2, issue, 0)
    pltpu.make_async_copy(rows_ref, out_hbm.at[pl.ds(0, n * ROW_SUB)], sem).wait()


def _scatter_rows(rows, dest3, ztile, n_out, tile):
    t = rows.shape[0] // ROW_SUB
    n = dest3.shape[-1]
    return pl.pallas_call(
        _scatter_kernel,
        grid=(t // n,),
        in_specs=[pl.BlockSpec((None, 1, n), lambda c: (c, 0, 0), memory_space=pltpu.SMEM),
                  pl.BlockSpec(ztile.shape, lambda c: (0, 0), memory_space=pltpu.SMEM),
                  pl.BlockSpec((n * ROW_SUB, LANES), lambda c: (c, 0))],
        out_specs=pl.BlockSpec(memory_space=pl.ANY),
        out_shape=jax.ShapeDtypeStruct((n_out * ROW_SUB, LANES), rows.dtype),
        scratch_shapes=[pltpu.VMEM((tile * ROW_SUB, LANES), F32), pltpu.SemaphoreType.DMA(()),
                        pltpu.SemaphoreType.DMA(())],
        compiler_params=_cparams(("arbitrary",)),
        name="scatter",
    )(dest3, ztile, rows)


def _unsort_kernel(dcur_ref, dnext_ref, rows_hbm, o_ref, buf, sem):
    c = pl.program_id(0)
    n = o_ref.shape[0]
    slot = c % 2

    def issue(dref, sl):
        def body(u, carry):
            for p in range(2):
                t = 2 * u + p
                pltpu.make_async_copy(rows_hbm.at[_row(dref[0, t])], buf.at[sl, _row(t)],
                                      sem.at[sl]).start(priority=p)
            return carry

        lax.fori_loop(0, n // 2, body, 0)

    @pl.when(c == 0)
    def _():
        issue(dcur_ref, 0)

    @pl.when(c + 1 < pl.num_programs(0))
    def _():
        issue(dnext_ref, 1 - slot)

    pltpu.make_async_copy(rows_hbm.at[pl.ds(0, n * ROW_SUB)], buf.at[slot], sem.at[slot]).wait()
    for k in range(ROW_SUB):
        o_ref[:, k * LANES:(k + 1) * LANES] = buf[slot, _lane_block(k, n), :]


def _unsort_rows(rows, dest3, n_tok):
    n = dest3.shape[-1]
    nc = n_tok // n
    return pl.pallas_call(
        _unsort_kernel,
        grid=(nc,),
        in_specs=[pl.BlockSpec((None, 1, n), lambda c: (c, 0, 0), memory_space=pltpu.SMEM),
                  pl.BlockSpec((None, 1, n), lambda c: (jnp.minimum(c + 1, nc - 1), 0, 0),
                               memory_space=pltpu.SMEM),
                  pl.BlockSpec(memory_space=pl.ANY)],
        out_specs=pl.BlockSpec((n, D_MODEL), lambda c: (c, 0)),
        out_shape=jax.ShapeDtypeStruct((n_tok, D_MODEL), rows.dtype),
        scratch_shapes=[pltpu.VMEM((2, n * ROW_SUB, LANES), F32), pltpu.SemaphoreType.DMA((2,))],
        compiler_params=_cparams(("arbitrary",)),
        name="unsort",
    )(dest3, dest3, rows)


def _moe_kernel(e1_ref, e2_ref, nused_ref, rows_ref, wr_ref, br_ref, wg1_ref, wu1_ref, wd1_ref,
                wg2_ref, wu2_ref, wd2_ref, g3_ref, b3_ref, o_ref):
    i = pl.program_id(0)
    tm = rows_ref.shape[0] // ROW_SUB

    @pl.when(i < nused_ref[0])
    def _():
        e1 = e1_ref[i]
        e2 = e2_ref[i]
        g = e1 // EXPERTS_PER_GROUP
        x = jnp.concatenate([rows_ref[_lane_block(k, tm), :] for k in range(ROW_SUB)], axis=1)
        xb = x.astype(BF16)

        logits = jnp.dot(xb, wr_ref[...], preferred_element_type=F32) + br_ref[...]
        lane = lax.broadcasted_iota(jnp.int32, (tm, LANES), 1)
        is_g = _is_group_lane(lane)
        pick = lambda idx: jnp.sum(jnp.where(lane == idx, logits, 0.0), axis=-1, keepdims=True)
        gmax = jnp.max(jnp.where(is_g, logits, -jnp.inf), axis=-1, keepdims=True)
        g_gate = jnp.exp(pick(N_EXPERTS + g) - gmax) / jnp.sum(jnp.where(is_g, jnp.exp(logits - gmax), 0.0),
                                                                axis=-1, keepdims=True)
        t1 = pick(e1)
        t2 = pick(e2)
        c1 = g_gate / (1.0 + jnp.exp(t2 - t1))
        c2 = g_gate / (1.0 + jnp.exp(t1 - t2))

        def expert(wg_ref, wu_ref, wd_ref, ce):
            hg = jnp.dot(xb, wg_ref[...], preferred_element_type=F32)
            hu = jnp.dot(xb, wu_ref[...], preferred_element_type=F32)
            hd = hg * _sigmoid(hg) * hu * ce
            return jnp.dot(hd.astype(BF16), wd_ref[...], preferred_element_type=F32)

        acc = expert(wg1_ref, wu1_ref, wd1_ref, c1) + expert(wg2_ref, wu2_ref, wd2_ref, c2)
        y = _layer_norm(ALPHA * x + acc, g3_ref[...], b3_ref[...])
        for k in range(ROW_SUB):
            o_ref[_lane_block(k, tm), :] = y[:, k * LANES:(k + 1) * LANES]

    @pl.when(i >= nused_ref[0])
    def _():
        o_ref[...] = jnp.zeros(o_ref.shape, F32)


def _moe(rows_sorted, tile_e1, tile_e2, n_used, wr, br, wg, wu, wd, g3, b3):
    n_rows = rows_sorted.shape[0] // ROW_SUB
    tm = TM_MOE
    nt = n_rows // tm

    def tile_map(i, e1, e2, nused):
        return (jnp.maximum(jnp.minimum(i, nused[0] - 1), 0), 0)

    first = lambda i, e1, e2, nused: (e1[i], 0, 0)
    second = lambda i, e1, e2, nused: (e2[i], 0, 0)
    full = lambda a: pl.BlockSpec(a.shape, lambda i, e1, e2, nused: (0,) * a.ndim)
    w_in = lambda m: pl.BlockSpec((None, D_MODEL, D_EXPERT), m)
    w_out = lambda m: pl.BlockSpec((None, D_EXPERT, D_MODEL), m)
    grid_spec = pltpu.PrefetchScalarGridSpec(
        num_scalar_prefetch=3,
        grid=(nt,),
        in_specs=[pl.BlockSpec((tm * ROW_SUB, LANES), tile_map), full(wr), full(br),
                  w_in(first), w_in(first), w_out(first), w_in(second), w_in(second), w_out(second),
                  full(g3), full(b3)],
        out_specs=pl.BlockSpec((tm * ROW_SUB, LANES), lambda i, e1, e2, nused: (i, 0)),
    )
    return pl.pallas_call(
        _moe_kernel,
        grid_spec=grid_spec,
        out_shape=jax.ShapeDtypeStruct((n_rows * ROW_SUB, LANES), F32),
        compiler_params=_cparams(("arbitrary",)),
        name="moe",
    )(tile_e1, tile_e2, n_used, rows_sorted, wr, br, wg, wu, wd, wg, wu, wd, g3, b3)


def _rope_tables(seq):
    pos = jnp.arange(seq, dtype=F32)
    inv_freq = ROPE_THETA ** (-jnp.arange(0, MLA_ROPE, 2, dtype=F32) / MLA_ROPE)
    ang = pos[:, None] * inv_freq[None, :]
    cos = jnp.tile(jnp.cos(ang), (1, LANES // (MLA_ROPE // 2)))
    sin = jnp.tile(jnp.sin(ang), (1, LANES // (MLA_ROPE // 2)))
    return cos, sin


def _pad_lanes(v, n=LANES):
    v = v.reshape(1, -1)
    return jnp.pad(v, ((0, 0), (0, n - v.shape[1])))


def kernel(x, mem, w_in, mla_q_norm, w_q_up, mla_kv_norm, w_kv_up, ssd_conv_w, ssd_conv_b, ssd_dt_bias, ssd_a_log, ssd_d, ssd_norm, w_out, ln1_g, ln1_b, xa_wq, xa_wk, xa_wv, xa_wo, ln2_g, ln2_b, router_group_w, router_group_b, router_expert_w, router_expert_b, expert_w_gate, expert_w_up, expert_w_down, ln3_g, ln3_b):
    b, s, d = x.shape
    t = b * s
    l = 0
    row = lambda v: v[l].reshape(1, -1)

    wi = w_in[l]
    o1 = MLA_Q_RANK
    o2 = o1 + MLA_KV_RANK
    o3 = o2 + MLA_ROPE
    o4 = o3 + SSD_INNER
    o5 = o4 + SSD_CONV_DIM
    w_kr = wi[:, o2:o3]
    half = MLA_ROPE // 2
    w_kr_rot = jnp.concatenate([-w_kr[:, half:], w_kr[:, :half]], axis=1)
    reps = LANES // MLA_ROPE
    w_in_r = jnp.concatenate(
        [wi[:, :o2], wi[:, o3:o4], wi[:, o4:o5], jnp.tile(w_kr, (1, reps)), jnp.tile(w_kr_rot, (1, reps)),
         wi[:, o5:], jnp.zeros((d, LANES - SSD_HEADS), F32)], axis=1).astype(BF16)
    cos4, sin4 = _rope_tables(s)
    attn_scale = (MLA_NOPE + MLA_ROPE) ** -0.5 * math.log2(math.e)
    w_abs, w_pe, w_per, w_mix = _fold_weights(w_q_up[l], w_kv_up[l], mla_kv_norm[l], w_out[l], attn_scale)
    expand = (jnp.arange(LANES)[:, None] == (jnp.arange(SSD_INNER)[None, :] // SSD_HEADDIM)).astype(BF16)
    d_full = jnp.repeat(ssd_d[l], SSD_HEADDIM).reshape(1, SSD_INNER)
    wrh, wrl = _split_bf16(jnp.concatenate(
        [router_expert_w[l], router_group_w[l], jnp.zeros((d, LANES - N_EXPERTS - N_EXPERT_GROUPS), F32)], axis=1))
    br = _pad_lanes(jnp.concatenate([router_expert_b[l], router_group_b[l]]))

    cq, kc, z, xbc, dt = _inproj(x, w_in_r, row(mla_q_norm), ssd_conv_w[l], row(ssd_conv_b),
                                 _pad_lanes(ssd_dt_bias[l]), cos4, sin4)
    o_lat = _attention(cq, kc, w_abs, w_pe, w_per, cos4, sin4)
    ssd_out = _ssd(xbc, dt, z, _pad_lanes(ssd_a_log[l]), d_full, row(ssd_norm), expand)

    kmem, vmem = _memkv(mem, xa_wk[l].astype(BF16), xa_wv[l].astype(BF16))
    wq = (xa_wq[l] * (XA_HEAD_DIM ** -0.5)).astype(BF16)
    rows, meta, counts = _mid(x, o_lat, ssd_out, w_mix, row(ln1_g), row(ln1_b), wq, kmem, vmem,
                              xa_wo[l].astype(BF16), row(ln2_g), row(ln2_b), wrh, wrl, br)

    tm = TM_MOE
    nt = t // tm + N_EXPERT_GROUPS * PAIRS_PER_GROUP - 1
    meta = meta.reshape(t, LANES)
    bucket = meta[:, 0].astype(jnp.int32)
    rank = meta[:, 1].astype(jnp.int32)
    cnt = counts[0].astype(jnp.int32)
    ntile = (cnt + tm - 1) // tm
    tile_end = jnp.cumsum(ntile)
    tile_start = tile_end - ntile
    first_row = jnp.sum(jnp.where(bucket[:, None] == jnp.arange(N_BUCKETS)[None, :], tile_start[None, :] * tm, 0),
                        axis=1)
    dest = first_row + rank
    n_used = tile_end[-1:].astype(jnp.int32)
    tile_bkt = jnp.minimum(jnp.sum(jnp.arange(nt)[:, None] >= tile_end[None, :], axis=1), N_BUCKETS - 1)
    pa, pb = zip(*[(a, c) for a in range(EXPERTS_PER_GROUP) for c in range(a + 1, EXPERTS_PER_GROUP)])
    bkt_g = jnp.minimum(tile_bkt // PAIRS_PER_GROUP, N_EXPERT_GROUPS - 1)
    tile_e1 = (bkt_g * EXPERTS_PER_GROUP + jnp.array(pa)[tile_bkt % PAIRS_PER_GROUP]).astype(jnp.int32)
    tile_e2 = (bkt_g * EXPERTS_PER_GROUP + jnp.array(pb)[tile_bkt % PAIRS_PER_GROUP]).astype(jnp.int32)
    spare = n_used[0] + jnp.arange(nt - t // tm)
    ztile = jnp.concatenate([jnp.where(ntile > 0, tile_end - 1, -1),
                             jnp.where(spare < nt, spare, -1)]).astype(jnp.int32).reshape(1, -1)
    sc_chunk = min(SCATTER_CHUNK, t)
    un_chunk = min(UNSORT_CHUNK, t)

    rows_sorted = _scatter_rows(rows.reshape(t * ROW_SUB, LANES), dest.reshape(t // sc_chunk, 1, sc_chunk),
                                ztile, nt * tm, tm)
    out_sorted = _moe(rows_sorted, tile_e1, tile_e2, n_used, wrh, br, expert_w_gate[l].astype(BF16),
                      expert_w_up[l].astype(BF16), expert_w_down[l].astype(BF16), row(ln3_g), row(ln3_b))
    out = _unsort_rows(out_sorted, dest.reshape(t // un_chunk, 1, un_chunk), t)
    return out.reshape(b, s, d)
```
